```python
import math
import jax
import jax.numpy as jnp
from jax import lax
import numpy as np

D_MODEL = 1024
BATCH = 32
SEQ = 256
DEPTH = 2
DEC_BATCH = 4
DEC_SEQ = 2048
PAST_LEN = 512

F32 = jnp.float32
GRID_W = 64
N_HEADS = 8
N_KV = 2
HEAD_DIM = 64
Q_GROUP = N_HEADS // N_KV
ATT_WIDTH = N_HEADS * HEAD_DIM
KV_WIDTH = N_KV * HEAD_DIM
ROPE_AXIS_DIM = HEAD_DIM // 2
ROPE_THETA = 10000.0
Q_BLOCK = 128
SC_WIDTH = D_MODEL - ATT_WIDTH
CONV_WIDTH = 3
C_HEADS = 8
C_DK = D_MODEL // C_HEADS
C_DV = D_MODEL // C_HEADS
C_WIDTH = C_HEADS * C_DK
CHUNK = 32
D_FF = 2816
N_ATTN_LAYERS = (DEPTH + 1) // 2
N_REC_LAYERS = DEPTH // 2
EVEN_IN_WIDTH = ATT_WIDTH + 2 * KV_WIDTH + 3 * SC_WIDTH
EVEN_SPLITS = (ATT_WIDTH, ATT_WIDTH + KV_WIDTH, ATT_WIDTH + 2 * KV_WIDTH,
               ATT_WIDTH + 2 * KV_WIDTH + SC_WIDTH, ATT_WIDTH + 2 * KV_WIDTH + 2 * SC_WIDTH)
ALPHA = (2 * DEPTH) ** 0.25
BETA = (8 * DEPTH) ** -0.25
EPS = 1e-6

kernel_name = 'hybrid_diffusion_prefix_trunk_step'


def _layer_norm(x, g, b):
    xf = x.astype(F32)
    mu = jnp.mean(xf, axis=-1, keepdims=True)
    var = jnp.mean(jnp.square(xf - mu), axis=-1, keepdims=True)
    return ((xf - mu) * lax.rsqrt(var + EPS) * g.astype(F32) + b.astype(F32)).astype(x.dtype)


def _rms_norm(x, g):
    xf = x.astype(F32)
    return (xf * lax.rsqrt(jnp.mean(xf * xf, axis=-1, keepdims=True) + EPS) * g.astype(F32)).astype(x.dtype)


def _dwconv(x, w):
    pad = CONV_WIDTH // 2
    return lax.conv_general_dilated(x, w.astype(x.dtype)[:, None, :], window_strides=(1,),
                                    padding=((pad, pad),), dimension_numbers=('NWC', 'WIO', 'NWC'),
                                    feature_group_count=x.shape[-1])


def _modulation(cond, w_mod, b_mod):
    return jnp.split(jax.nn.silu(cond) @ w_mod + b_mod, 6, axis=-1)


def _axial_rope_tables(n_tokens):
    rows = n_tokens // GRID_W
    row_idx = jnp.repeat(jnp.arange(rows, dtype=F32), GRID_W)
    col_idx = jnp.tile(jnp.arange(GRID_W, dtype=F32), rows)
    inv = ROPE_THETA ** (-jnp.arange(0, ROPE_AXIS_DIM, 2, dtype=F32) / ROPE_AXIS_DIM)
    ang = jnp.concatenate([row_idx[:, None] * inv, col_idx[:, None] * inv], axis=-1)
    return jnp.cos(ang), jnp.sin(ang)


def _apply_axial_rope(x, cos, sin):
    half = ROPE_AXIS_DIM // 2
    xf = x.astype(F32)

    def rot(xa, c, s):
        c = c[None, :, None, :]
        s = s[None, :, None, :]
        x1, x2 = xa[..., :half], xa[..., half:]
        return jnp.concatenate([x1 * c - x2 * s, x1 * s + x2 * c], axis=-1)

    out = jnp.concatenate([rot(xf[..., :ROPE_AXIS_DIM], cos[:, :half], sin[:, :half]),
                           rot(xf[..., ROPE_AXIS_DIM:], cos[:, half:], sin[:, half:])], axis=-1)
    return out.astype(x.dtype)


def _attend(q, k, v):
    B, T = q.shape[0], q.shape[1]
    nb = T // Q_BLOCK
    qb = q.reshape(B, nb, Q_BLOCK, N_KV, Q_GROUP, HEAD_DIM).transpose(1, 0, 2, 3, 4, 5)
    scale = HEAD_DIM ** -0.5

    def block(qblk):
        s = jnp.einsum('bqkgd,bskd->bkgqs', qblk, k).astype(F32) * scale
        p = jax.nn.softmax(s, axis=-1).astype(v.dtype)
        return jnp.einsum('bkgqs,bskd->bqkgd', p, v)

    o = lax.map(block, qb)
    return o.transpose(1, 0, 2, 3, 4, 5).reshape(B, T, ATT_WIDTH)


def _attn_conv_project(h, w_in, q_gain, k_gain):
    B, T, _ = h.shape
    q, k, v, bg, cg, xi = jnp.split(h @ w_in, EVEN_SPLITS, axis=-1)
    q = _rms_norm(q.reshape(B, T, N_HEADS, HEAD_DIM), q_gain)
    k = _rms_norm(k.reshape(B, T, N_KV, HEAD_DIM), k_gain)
    v = v.reshape(B, T, N_KV, HEAD_DIM)
    return q, k, v, bg, cg, xi


def _attn_conv_context(h, w_in, q_gain, k_gain, conv_w, w_out):
    q, k, v, bg, cg, xi = _attn_conv_project(h, w_in, q_gain, k_gain)
    att = _attend(q, k, v)
    sc = bg * _dwconv(cg * xi, conv_w)
    return jnp.concatenate([att, sc], axis=-1) @ w_out, k, v


def _attn_conv_latent(h, w_in, q_gain, k_gain, conv_w, w_out, ctx_k, ctx_v, cos, sin):
    q, k, v, bg, cg, xi = _attn_conv_project(h, w_in, q_gain, k_gain)
    q = _apply_axial_rope(q, cos, sin)
    k = _apply_axial_rope(k, cos, sin)
    keys = jnp.concatenate([k, ctx_k.astype(k.dtype)], axis=1)
    vals = jnp.concatenate([v, ctx_v.astype(v.dtype)], axis=1)
    att = _attend(q, keys, vals)
    sc = bg * _dwconv(cg * xi, conv_w)
    return jnp.concatenate([att, sc], axis=-1) @ w_out


def _chunk_scan(q, log_f, k, v, s0):
    B, T, H, _ = q.shape
    DV = v.shape[-1]
    nc = T // CHUNK

    def to_chunks(a):
        return a.reshape(B, nc, CHUNK, H, a.shape[-1]).transpose(1, 0, 3, 2, 4)

    lower = jnp.tril(jnp.ones((CHUNK, CHUNK), dtype=bool))[:, :, None]

    def step(S, blk):
        qb, gb, kb, vb = blk
        b = jnp.cumsum(gb, axis=2)
        o_inter = jnp.einsum('bhtk,bhkv->bhtv', qb * jnp.exp(b), S)
        diff = b[:, :, :, None, :] - b[:, :, None, :, :]
        decay = jnp.exp(jnp.where(lower, diff, -jnp.inf))
        scores = jnp.einsum('bhtk,bhsk,bhtsk->bhts', qb, kb, decay)
        o = o_inter + jnp.einsum('bhts,bhsv->bhtv', scores, vb)
        b_last = b[:, :, -1:, :]
        S_new = (jnp.exp(b_last[:, :, 0, :])[..., None] * S
                 + jnp.einsum('bhsk,bhsv->bhkv', kb * jnp.exp(b_last - b), vb))
        return S_new, o

    S, o = lax.scan(step, s0, (to_chunks(q), to_chunks(log_f), to_chunks(k), to_chunks(v)))
    return o.transpose(1, 0, 3, 2, 4).reshape(B, T, H, DV), S


def _hgrn_mixer(h, w_in, lb, norm_g, w_out, s_fw, s_bw):
    B, T, _ = h.shape
    z = (h @ w_in).astype(F32)
    q, i, g, f_fw, f_bw = jnp.split(z, 5, axis=-1)

    def heads(a):
        return a.reshape(B, T, C_HEADS, -1)

    def decay(f_raw, lb_d):
        f = lb_d + (1.0 - lb_d) * jax.nn.sigmoid(f_raw)
        return heads(jnp.log(f)), heads(1.0 - f)

    q, i = heads(q), heads(i)
    g_fw, k_fw = decay(f_fw, lb[0])
    g_bw, k_bw = decay(f_bw, lb[1])
    o_f, S_f = _chunk_scan(q, g_fw, k_fw, i, s_fw)
    o_b, S_b = _chunk_scan(q[:, ::-1], g_bw[:, ::-1], k_bw[:, ::-1], i[:, ::-1], s_bw)
    o = _rms_norm(o_f + o_b[:, ::-1], norm_g).reshape(B, T, C_WIDTH) * jax.nn.silu(g)
    return o.astype(h.dtype) @ w_out, S_f, S_b


def _conv_ffn(h, w_up, conv_w, w_down):
    u = _dwconv(h @ w_up, conv_w)
    a, gt = jnp.split(u, 2, axis=-1)
    return (jax.nn.silu(gt) * a) @ w_down


def setup_inputs(seed: int = 0) -> dict:
    key = jax.random.key(seed)
    ks = jax.random.split(key, 25)

    def nrm(idx, shape, scale):
        return jax.random.normal(ks[idx], shape, F32) * scale

    d_in = D_MODEL ** -0.5
    return {
        'x_prompt': nrm(0, (BATCH, SEQ, D_MODEL), 1.0),
        'x_sample': nrm(1, (DEC_BATCH, DEC_SEQ, D_MODEL), 1.0),
        'cache_k': nrm(2, (DEC_BATCH, N_ATTN_LAYERS, PAST_LEN, N_KV, HEAD_DIM), 1.0),
        'cache_v': nrm(3, (DEC_BATCH, N_ATTN_LAYERS, PAST_LEN, N_KV, HEAD_DIM), 1.0),
        'state_hgrn': nrm(4, (DEC_BATCH, N_REC_LAYERS, 2, C_HEADS, C_DK, C_DV), 0.5),
        'c': nrm(5, (DEC_BATCH, D_MODEL), 1.0),
        'c_ctx': nrm(6, (D_MODEL,), 1.0),
        'w_mod': nrm(7, (DEPTH, D_MODEL, 6 * D_MODEL), 0.5 * d_in),
        'b_mod': nrm(8, (DEPTH, 6 * D_MODEL), 0.02),
        'ln1_g': 1.0 + nrm(9, (DEPTH, D_MODEL), 0.02),
        'ln1_b': nrm(10, (DEPTH, D_MODEL), 0.02),
        'ln2_g': 1.0 + nrm(11, (DEPTH, D_MODEL), 0.02),
        'ln2_b': nrm(12, (DEPTH, D_MODEL), 0.02),
        'attn_w_in': nrm(13, (N_ATTN_LAYERS, D_MODEL, EVEN_IN_WIDTH), d_in),
        'attn_q_gain': 1.0 + nrm(14, (N_ATTN_LAYERS, HEAD_DIM), 0.02),
        'attn_k_gain': 1.0 + nrm(15, (N_ATTN_LAYERS, HEAD_DIM), 0.02),
        'sconv_w': nrm(16, (N_ATTN_LAYERS, CONV_WIDTH, SC_WIDTH), CONV_WIDTH ** -0.5),
        'attn_w_out': nrm(17, (N_ATTN_LAYERS, ATT_WIDTH + SC_WIDTH, D_MODEL), (ATT_WIDTH + SC_WIDTH) ** -0.5 * BETA),
        'hgrn_w_in': nrm(18, (N_REC_LAYERS, D_MODEL, 5 * C_WIDTH), d_in),
        'hgrn_lb_logits': nrm(19, (DEPTH, 2, C_WIDTH), 0.5),
        'hgrn_norm_g': 1.0 + nrm(20, (N_REC_LAYERS, C_DV), 0.02),
        'hgrn_w_out': nrm(21, (N_REC_LAYERS, C_WIDTH, D_MODEL), C_WIDTH ** -0.5 * BETA),
        'ffn_w_up': nrm(22, (DEPTH, D_MODEL, 2 * D_FF), d_in),
        'ffn_conv_w': nrm(23, (DEPTH, CONV_WIDTH, 2 * D_FF), CONV_WIDTH ** -0.5),
        'ffn_w_down': nrm(24, (DEPTH, D_FF, D_MODEL), D_FF ** -0.5 * BETA),
    }


def reference(x_prompt, x_sample, cache_k, cache_v, state_hgrn, c, c_ctx,
              w_mod, b_mod, ln1_g, ln1_b, ln2_g, ln2_b,
              attn_w_in, attn_q_gain, attn_k_gain, sconv_w, attn_w_out,
              hgrn_w_in, hgrn_lb_logits, hgrn_norm_g, hgrn_w_out,
              ffn_w_up, ffn_conv_w, ffn_w_down):
    lb_all = jnp.cumsum(jax.nn.softmax(hgrn_lb_logits.astype(F32), axis=0), axis=0)
    lb_all = lb_all - lb_all[0]
    cos, sin = _axial_rope_tables(x_sample.shape[1])
    cond_ctx = c_ctx[None, None, :]
    cond_lat = c[:, None, :]
    zero_state = jnp.zeros((x_prompt.shape[0], C_HEADS, C_DK, C_DV), F32)
    xp, xs = x_prompt, x_sample
    new_k, new_v, new_s = [], [], []
    for l in range(DEPTH):
        j = l // 2
        sh1_p, sc1_p, gt1_p, sh2_p, sc2_p, gt2_p = _modulation(cond_ctx, w_mod[l], b_mod[l])
        sh1_s, sc1_s, gt1_s, sh2_s, sc2_s, gt2_s = _modulation(cond_lat, w_mod[l], b_mod[l])
        hp = xp * (1 + sc1_p) + sh1_p
        hs = xs * (1 + sc1_s) + sh1_s
        if l % 2 == 0:
            op, kc, vc = _attn_conv_context(hp, attn_w_in[j], attn_q_gain[j], attn_k_gain[j],
                                            sconv_w[j], attn_w_out[j])
            os_ = _attn_conv_latent(hs, attn_w_in[j], attn_q_gain[j], attn_k_gain[j],
                                    sconv_w[j], attn_w_out[j], cache_k[:, j], cache_v[:, j], cos, sin)
            new_k.append(kc)
            new_v.append(vc)
        else:
            op, sf, sb = _hgrn_mixer(hp, hgrn_w_in[j], lb_all[l], hgrn_norm_g[j], hgrn_w_out[j],
                                     zero_state, zero_state)
            os_, _, _ = _hgrn_mixer(hs, hgrn_w_in[j], lb_all[l], hgrn_norm_g[j], hgrn_w_out[j],
                                    state_hgrn[:, j, 0].astype(F32), state_hgrn[:, j, 1].astype(F32))
            new_s.append(jnp.stack([sf, sb], axis=1))
        xp = _layer_norm(ALPHA * xp + gt1_p * op, ln1_g[l], ln1_b[l])
        xs = _layer_norm(ALPHA * xs + gt1_s * os_, ln1_g[l], ln1_b[l])
        fp = _conv_ffn(xp * (1 + sc2_p) + sh2_p, ffn_w_up[l], ffn_conv_w[l], ffn_w_down[l])
        fs = _conv_ffn(xs * (1 + sc2_s) + sh2_s, ffn_w_up[l], ffn_conv_w[l], ffn_w_down[l])
        xp = _layer_norm(ALPHA * xp + gt2_p * fp, ln2_g[l], ln2_b[l])
        xs = _layer_norm(ALPHA * xs + gt2_s * fs, ln2_g[l], ln2_b[l])
    new_cache_k = jnp.stack(new_k, axis=1)
    new_cache_v = jnp.stack(new_v, axis=1)
    new_state_hgrn = jnp.stack(new_s, axis=1)
    return (xp, xs, new_cache_k, new_cache_v, new_state_hgrn)
```

```python
import functools

import jax
import jax.numpy as jnp
import numpy as np
from jax import lax
from jax.experimental import pallas as pl
from jax.experimental.pallas import tpu as pltpu

F32 = jnp.float32
BF16 = jnp.bfloat16

D_MODEL = 1024
BATCH = 32
SEQ = 256
DEPTH = 2
DEC_BATCH = 4
DEC_SEQ = 2048
PAST_LEN = 512
GRID_W = 64
N_HEADS = 8
N_KV = 2
HEAD_DIM = 64
Q_GROUP = N_HEADS // N_KV
ATT_WIDTH = N_HEADS * HEAD_DIM
KV_WIDTH = N_KV * HEAD_DIM
ROPE_AXIS_DIM = HEAD_DIM // 2
ROPE_THETA = 10000.0
SC_WIDTH = D_MODEL - ATT_WIDTH
CONV_WIDTH = 3
C_HEADS = 8
C_DK = D_MODEL // C_HEADS
C_DV = D_MODEL // C_HEADS
C_WIDTH = C_HEADS * C_DK
D_FF = 2816
EVEN_IN_WIDTH = ATT_WIDTH + 2 * KV_WIDTH + 3 * SC_WIDTH
ALPHA = (2 * DEPTH) ** 0.25
EPS = 1e-6

P_TOK = BATCH * SEQ
S_TOK = DEC_BATCH * DEC_SEQ
N_TOK = P_TOK + S_TOK
MOD_ROWS = 8
TM = 512
HALO = 8
FF_CHUNK = 256
SCAN_CHUNK = 128
TQ = 256
VMEM_LIMIT = 56 * 1024 * 1024

assert P_TOK % TM == 0 and S_TOK % TM == 0 and DEC_SEQ % TM == 0 and TM % SEQ == 0
assert D_FF % FF_CHUNK == 0 and SEQ % SCAN_CHUNK == 0 and DEC_SEQ % SCAN_CHUNK == 0
P_BLOCKS = P_TOK // TM
N_BLOCKS = N_TOK // TM
S_BLOCKS_PER_SEQ = DEC_SEQ // TM


def _params(*sem):
    return pltpu.CompilerParams(dimension_semantics=sem, vmem_limit_bytes=VMEM_LIMIT)


def _mod_row(i):
    return jnp.where(i < P_BLOCKS, 0, 1 + (i - P_BLOCKS) // S_BLOCKS_PER_SEQ)


def _resident(shape):
    nd = len(shape)
    return pl.BlockSpec(shape, lambda *_: (0,) * nd, pipeline_mode=pl.Buffered(1))


def _sigmoid(x):
    return 1.0 / (1.0 + jnp.exp(-x))


def _silu(x):
    return x * _sigmoid(x)


def _layer_norm(x, g, b):
    mu = jnp.mean(x, axis=-1, keepdims=True)
    xc = x - mu
    var = jnp.mean(xc * xc, axis=-1, keepdims=True)
    return xc * lax.rsqrt(var + EPS) * g + b


def _group_mean_matrix(group):
    idx = np.arange(256)
    return jnp.asarray((idx[:, None] // group == idx[None, :] // group) / group, dtype=BF16)


def _group_mean_square(x, gmat_ref):
    sq = (x * x).astype(BF16)
    n = x.shape[1] // 256
    parts = [jnp.dot(sq[:, 256 * j:256 * (j + 1)], gmat_ref[...], preferred_element_type=F32) for j in range(n)]
    return parts[0] if n == 1 else jnp.concatenate(parts, axis=1)


def _seq_pos(i, rows, first_row):
    seq_len = jnp.where(i < P_BLOCKS, SEQ, DEC_SEQ)
    r = lax.broadcasted_iota(jnp.int32, (rows, 1), 0) + (i * TM + first_row)
    return r & (seq_len - 1), seq_len


def _shifted_rows(u, keep_prev, keep_next):
    n = u.shape[0]
    return pltpu.roll(u, 1, 0) * keep_prev, pltpu.roll(u, n - 1, 0) * keep_next


MOD_TN = 1536


def _mod_kernel(cond_ref, w_ref, b_ref, o_ref):
    s = _silu(cond_ref[...])
    o_ref[...] = jnp.dot(s, w_ref[...], preferred_element_type=F32) + b_ref[...]


def _modulation(cond, w_mod, b_mod):
    n_out = 6 * D_MODEL
    return pl.pallas_call(
        _mod_kernel,
        grid=(DEPTH, n_out // MOD_TN),
        in_specs=[pl.BlockSpec((MOD_ROWS, D_MODEL), lambda l, j: (0, 0)),
                  pl.BlockSpec((None, D_MODEL, MOD_TN), lambda l, j: (l, 0, j)),
                  pl.BlockSpec((None, 1, MOD_TN), lambda l, j: (l, 0, j))],
        out_specs=pl.BlockSpec((None, MOD_ROWS, MOD_TN), lambda l, j: (l, 0, j)),
        out_shape=jax.ShapeDtypeStruct((DEPTH, MOD_ROWS, n_out), F32),
        compiler_params=_params("arbitrary", "arbitrary"),
        name="modulation",
    )(cond, w_mod, b_mod.reshape(DEPTH, 1, n_out))


def _rope_tables():
    t = np.arange(DEC_SEQ)
    half = ROPE_AXIS_DIM // 2
    inv = (ROPE_THETA ** (-np.arange(0, ROPE_AXIS_DIM, 2, dtype=np.float32) / ROPE_AXIS_DIM)).astype(np.float32)
    row = (t // GRID_W).astype(np.float32)[:, None] * inv
    col = (t % GRID_W).astype(np.float32)[:, None] * inv
    ang = np.concatenate([row, row, col, col], axis=1).astype(np.float32)
    sign = np.concatenate([-np.ones(half), np.ones(half)] * 2).astype(np.float32)
    cos = np.tile(np.cos(ang), (1, 128 // HEAD_DIM))
    sin = np.tile(np.sin(ang) * sign, (1, 128 // HEAD_DIM))
    return jnp.asarray(cos, F32), jnp.asarray(sin, F32)


def _rope(x, cos, sin):
    n = x.shape[1] // 128
    if n > 1:
        cos = jnp.concatenate([cos] * n, axis=1)
        sin = jnp.concatenate([sin] * n, axis=1)
    lane = lax.broadcasted_iota(jnp.int32, x.shape, 1)
    half = ROPE_AXIS_DIM // 2
    partner = jnp.where((lane & (ROPE_AXIS_DIM - 1)) < half,
                        pltpu.roll(x, x.shape[1] - half, 1), pltpu.roll(x, half, 1))
    return x * cos + partner * sin


def _attn_in_kernel(x_ref, mod_ref, w_ref, qg_ref, kg_ref, gmat_ref, cos_ref, sin_ref,
                    q_ref, k_ref, v_ref, bg_ref, u_ref):
    i = pl.program_id(0)
    m = mod_ref[0]
    h = (x_ref[...] * (1.0 + m[1:2]) + m[0:1]).astype(BF16)
    z = jnp.dot(h, w_ref[...], preferred_element_type=F32)
    q = z[:, :ATT_WIDTH]
    k = z[:, ATT_WIDTH:ATT_WIDTH + KV_WIDTH]
    o = ATT_WIDTH + 2 * KV_WIDTH
    q = q * lax.rsqrt(_group_mean_square(q, gmat_ref) + EPS) * qg_ref[...]
    kk = (k * k).astype(BF16)
    k_ms = jnp.dot(kk, gmat_ref[:KV_WIDTH, :KV_WIDTH], preferred_element_type=F32)
    k = k * lax.rsqrt(k_ms + EPS) * kg_ref[...]
    v_ref[...] = z[:, ATT_WIDTH + KV_WIDTH:o]
    bg_ref[...] = z[:, o:o + SC_WIDTH].astype(BF16)
    u_ref[...] = z[:, o + SC_WIDTH:o + 2 * SC_WIDTH] * z[:, o + 2 * SC_WIDTH:]

    @pl.when(i < P_BLOCKS)
    def _():
        q_ref[...] = (q * HEAD_DIM ** -0.5).astype(BF16)
        k_ref[...] = k

    @pl.when(i >= P_BLOCKS)
    def _():
        cos, sin = cos_ref[...], sin_ref[...]
        q_ref[...] = (_rope(q, cos, sin) * HEAD_DIM ** -0.5).astype(BF16)
        k_ref[...] = _rope(k, cos, sin)


def _attn_in(x, mod, w_in, q_gain, k_gain):
    cos, sin = _rope_tables()
    row = lambda i: (i, 0)
    rope_row = lambda i: (jnp.maximum(i - P_BLOCKS, 0) % S_BLOCKS_PER_SEQ, 0)
    tok = lambda w, dt: jax.ShapeDtypeStruct((N_TOK, w), dt)
    return pl.pallas_call(
        _attn_in_kernel,
        grid=(N_BLOCKS,),
        in_specs=[pl.BlockSpec((TM, D_MODEL), row),
                  pl.BlockSpec((1, 6, D_MODEL), lambda i: (_mod_row(i), 0, 0)),
                  _resident((D_MODEL, EVEN_IN_WIDTH)),
                  _resident((1, ATT_WIDTH)), _resident((1, KV_WIDTH)), _resident((256, 256)),
                  pl.BlockSpec((TM, 128), rope_row), pl.BlockSpec((TM, 128), rope_row)],
        out_specs=[pl.BlockSpec((TM, ATT_WIDTH), row), pl.BlockSpec((TM, KV_WIDTH), row),
                   pl.BlockSpec((TM, KV_WIDTH), row), pl.BlockSpec((TM, SC_WIDTH), row),
                   pl.BlockSpec((TM, SC_WIDTH), row)],
        out_shape=[tok(ATT_WIDTH, BF16), tok(KV_WIDTH, F32), tok(KV_WIDTH, F32), tok(SC_WIDTH, BF16),
                   tok(SC_WIDTH, F32)],
        compiler_params=_params("arbitrary"),
        name="attn_in",
    )(x, mod, w_in.astype(BF16), jnp.tile(q_gain, N_HEADS)[None], jnp.tile(k_gain, N_KV)[None],
      _group_mean_matrix(HEAD_DIM), cos, sin)


def _attend_kernel(q_ref, *refs, n_kv_sets):
    kv_refs, o_ref = refs[:2 * n_kv_sets], refs[2 * n_kv_sets]
    ks = [r[...].astype(BF16) for r in kv_refs[:n_kv_sets]]
    vs = [r[...].astype(BF16) for r in kv_refs[n_kv_sets:]]
    q = q_ref[...]
    nt = (((1,), (1,)), ((), ()))
    for hd in range(N_HEADS):
        g = hd // Q_GROUP
        qh = q[:, hd * HEAD_DIM:(hd + 1) * HEAD_DIM]
        s = [lax.dot_general(qh, kx[:, g * HEAD_DIM:(g + 1) * HEAD_DIM], nt, preferred_element_type=F32)
             for kx in ks]
        mx = functools.reduce(jnp.maximum, [jnp.max(sx, axis=-1, keepdims=True) for sx in s])
        p = [jnp.exp(sx - mx) for sx in s]
        den = functools.reduce(jnp.add, [jnp.sum(px, axis=-1, keepdims=True) for px in p])
        acc = functools.reduce(jnp.add, [
            jnp.dot(px.astype(BF16), vx[:, g * HEAD_DIM:(g + 1) * HEAD_DIM], preferred_element_type=F32)
            for px, vx in zip(p, vs)])
        o_ref[:, hd * HEAD_DIM:(hd + 1) * HEAD_DIM] = (acc / den).astype(BF16)


def _attend_prompt(q, k, v):
    blk = lambda w: pl.BlockSpec((SEQ, w), lambda b: (b, 0))
    return pl.pallas_call(
        functools.partial(_attend_kernel, n_kv_sets=1),
        grid=(BATCH,),
        in_specs=[blk(ATT_WIDTH), blk(KV_WIDTH), blk(KV_WIDTH)],
        out_specs=blk(ATT_WIDTH),
        out_shape=jax.ShapeDtypeStruct((P_TOK, ATT_WIDTH), BF16),
        compiler_params=_params("arbitrary"),
        name="attend_prompt",
    )(q, k, v)


def _attend_latent(q, k, v, ctx_k, ctx_v):
    nq = DEC_SEQ // TQ
    q_off, kv_off = P_TOK // TQ, P_TOK // DEC_SEQ
    lat = pl.BlockSpec((DEC_SEQ, KV_WIDTH), lambda b, j: (kv_off + b, 0))
    ctx = pl.BlockSpec((None, PAST_LEN, KV_WIDTH), lambda b, j: (b, 0, 0))
    return pl.pallas_call(
        functools.partial(_attend_kernel, n_kv_sets=2),
        grid=(DEC_BATCH, nq),
        in_specs=[pl.BlockSpec((TQ, ATT_WIDTH), lambda b, j: (q_off + b * nq + j, 0)), lat, ctx, lat, ctx],
        out_specs=pl.BlockSpec((TQ, ATT_WIDTH), lambda b, j: (b * nq + j, 0)),
        out_shape=jax.ShapeDtypeStruct((S_TOK, ATT_WIDTH), BF16),
        compiler_params=_params("arbitrary", "arbitrary"),
        name="attend_latent",
    )(q, k, ctx_k, v, ctx_v)


def _residual_norm(x, y, gate, g, b):
    return _layer_norm(ALPHA * x + gate * y, g, b)


def _halo_specs(width):
    per = TM // HALO
    prev = pl.BlockSpec((HALO, width), lambda i: (jnp.maximum(i * per - 1, 0), 0))
    nxt = pl.BlockSpec((HALO, width), lambda i: (jnp.minimum((i + 1) * per, N_TOK // HALO - 1), 0))
    return prev, nxt


def _attn_out_kernel(x_ref, mod_ref, att_ref, bg_ref, u_ref, up_ref, un_ref, cw_ref, w_ref, g_ref, b_ref, o_ref):
    i = pl.program_id(0)
    m = mod_ref[0]
    pos, seq_len = _seq_pos(i, TM, 0)
    r = lax.broadcasted_iota(jnp.int32, (TM, 1), 0)
    u = u_ref[...]
    u_prev = jnp.where(r == 0, up_ref[HALO - 1:HALO, :], pltpu.roll(u, 1, 0))
    u_next = jnp.where(r == TM - 1, un_ref[0:1, :], pltpu.roll(u, TM - 1, 0))
    u_prev = jnp.where(pos == 0, 0.0, u_prev)
    u_next = jnp.where(pos == seq_len - 1, 0.0, u_next)
    cw = cw_ref[...]
    sc = bg_ref[...].astype(F32) * (cw[0:1] * u_prev + cw[1:2] * u + cw[2:3] * u_next)
    y = (jnp.dot(att_ref[...], w_ref[:ATT_WIDTH, :], preferred_element_type=F32)
         + jnp.dot(sc.astype(BF16), w_ref[ATT_WIDTH:, :], preferred_element_type=F32))
    o_ref[...] = _residual_norm(x_ref[...], y, m[2:3], g_ref[...], b_ref[...])


def _attn_out(x, mod, att, bg, u, conv_w, w_out, g, b):
    row = lambda i: (i, 0)
    up, un = _halo_specs(SC_WIDTH)
    return pl.pallas_call(
        _attn_out_kernel,
        grid=(N_BLOCKS,),
        in_specs=[pl.BlockSpec((TM, D_MODEL), row),
                  pl.BlockSpec((1, 6, D_MODEL), lambda i: (_mod_row(i), 0, 0)),
                  pl.BlockSpec((TM, ATT_WIDTH), row), pl.BlockSpec((TM, SC_WIDTH), row),
                  pl.BlockSpec((TM, SC_WIDTH), row), up, un,
                  _resident((CONV_WIDTH, SC_WIDTH)), _resident((D_MODEL, D_MODEL)),
                  _resident((1, D_MODEL)), _resident((1, D_MODEL))],
        out_specs=pl.BlockSpec((TM, D_MODEL), row),
        out_shape=jax.ShapeDtypeStruct((N_TOK, D_MODEL), F32),
        compiler_params=_params("arbitrary"),
        name="attn_out",
    )(x, mod, att, bg, u, u, u, conv_w, w_out.astype(BF16), g[None], b[None])


def _ffn_kernel(x_ref, xp_ref, xn_ref, mod_ref, wu_ref, cw_ref, wd_ref, g_ref, b_ref, o_ref, h_ref, acc_ref):
    i = pl.program_id(0)
    m = mod_ref[0]
    rows = TM + 2 * HALO
    scale, shift = 1.0 + m[4:5], m[3:4]
    h_ref[:HALO, :] = (xp_ref[...] * scale + shift).astype(BF16)
    h_ref[HALO:HALO + TM, :] = (x_ref[...] * scale + shift).astype(BF16)
    h_ref[HALO + TM:, :] = (xn_ref[...] * scale + shift).astype(BF16)
    pos, seq_len = _seq_pos(i, rows, -HALO)
    keep_prev = (pos != 0).astype(F32)
    keep_next = (pos != seq_len - 1).astype(F32)
    acc_ref[...] = jnp.zeros_like(acc_ref)

    def conv(u, col):
        w = cw_ref[:, pl.ds(col, FF_CHUNK)]
        u_prev, u_next = _shifted_rows(u, keep_prev, keep_next)
        return (w[0:1] * u_prev + w[1:2] * u + w[2:3] * u_next)[HALO:HALO + TM]

    def step(c, carry):
        col_a = pl.multiple_of(c * FF_CHUNK, FF_CHUNK)
        col_g = pl.multiple_of(D_FF + c * FF_CHUNK, FF_CHUNK)
        h = h_ref[...]
        a = conv(jnp.dot(h, wu_ref[:, pl.ds(col_a, FF_CHUNK)], preferred_element_type=F32), col_a)
        gt = conv(jnp.dot(h, wu_ref[:, pl.ds(col_g, FF_CHUNK)], preferred_element_type=F32), col_g)
        act = (_silu(gt) * a).astype(BF16)
        acc_ref[...] += jnp.dot(act, wd_ref[pl.ds(col_a, FF_CHUNK), :], preferred_element_type=F32)
        return carry

    lax.fori_loop(0, D_FF // FF_CHUNK, step, 0)
    o_ref[...] = _residual_norm(x_ref[...], acc_ref[...], m[5:6], g_ref[...], b_ref[...])


def _ffn(x, mod, w_up, conv_w, w_down, g, b):
    row = lambda i: (i, 0)
    xp, xn = _halo_specs(D_MODEL)
    return pl.pallas_call(
        _ffn_kernel,
        grid=(N_BLOCKS,),
        in_specs=[pl.BlockSpec((TM, D_MODEL), row), xp, xn,
                  pl.BlockSpec((1, 6, D_MODEL), lambda i: (_mod_row(i), 0, 0)),
                  _resident((D_MODEL, 2 * D_FF)), _resident((CONV_WIDTH, 2 * D_FF)), _resident((D_FF, D_MODEL)),
                  _resident((1, D_MODEL)), _resident((1, D_MODEL))],
        out_specs=pl.BlockSpec((TM, D_MODEL), row),
        out_shape=jax.ShapeDtypeStruct((N_TOK, D_MODEL), F32),
        scratch_shapes=[pltpu.VMEM((TM + 2 * HALO, D_MODEL), BF16), pltpu.VMEM((TM, D_MODEL), F32)],
        compiler_params=_params("arbitrary"),
        name="conv_ffn",
    )(x, x, x, mod, w_up.astype(BF16), conv_w, w_down.astype(BF16), g[None], b[None])


def _hgrn_in_kernel(x_ref, mod_ref, w_ref, z_ref):
    m = mod_ref[0]
    h = (x_ref[...] * (1.0 + m[1:2]) + m[0:1]).astype(BF16)
    z_ref[...] = jnp.dot(h, w_ref[...], preferred_element_type=F32)


def _hgrn_in(x, mod, w_in):
    n_col = 5 * C_WIDTH // C_WIDTH
    return pl.pallas_call(
        _hgrn_in_kernel,
        grid=(N_BLOCKS, n_col),
        in_specs=[pl.BlockSpec((TM, D_MODEL), lambda i, j: (i, 0)),
                  pl.BlockSpec((1, 6, D_MODEL), lambda i, j: (_mod_row(i), 0, 0)),
                  pl.BlockSpec((D_MODEL, C_WIDTH), lambda i, j: (0, j))],
        out_specs=pl.BlockSpec((TM, C_WIDTH), lambda i, j: (i, j)),
        out_shape=jax.ShapeDtypeStruct((N_TOK, 5 * C_WIDTH), F32),
        compiler_params=_params("arbitrary", "arbitrary"),
        name="hgrn_in",
    )(x, mod, w_in.astype(BF16))


SCAN_LEVELS = tuple(1 << j for j in range(SCAN_CHUNK.bit_length() - 1))


def _split3(x):
    a = x.astype(BF16)
    r = x - a.astype(F32)
    b = r.astype(BF16)
    c = (r - b.astype(F32)).astype(BF16)
    return a, b, c


def _ref_rows(b, half, forward):
    C = SCAN_CHUNK
    off = half - 1 if forward else half
    if half >= 8:
        blk = 2 * half
        return jnp.concatenate([jnp.broadcast_to(b[j * blk + off:j * blk + off + 1, :], (blk, C_DK))
                                for j in range(C // blk)], axis=0)
    b3 = b.reshape(C // 8, 8, C_DK)
    sub = lax.broadcasted_iota(jnp.int32, b3.shape, 1)
    blk = 2 * half
    out = None
    for j in range(8 // blk):
        cand = jnp.broadcast_to(b3[:, j * blk + off:j * blk + off + 1, :], b3.shape)
        out = cand if out is None else jnp.where(sub >= j * blk, cand, out)
    return out.reshape(C, C_DK)


def _pair_levels(forward):
    idx = np.arange(SCAN_CHUNK)
    t, s = idx[:, None], idx[None, :]
    lev = np.floor(np.log2(np.maximum(t ^ s, 1))).astype(np.int32)
    causal = t > s if forward else t < s
    return jnp.asarray(np.where(causal, lev, -1), dtype=jnp.int32)


def _intra_chunk(q, k, cum, lev, forward):
    C = SCAN_CHUNK
    t = lax.broadcasted_iota(jnp.int32, (C, 1), 0)
    nt = (((1,), (1,)), ((), ()))
    scores = jnp.zeros((C, C), F32)
    for j, half in enumerate(SCAN_LEVELS):
        e = cum - _ref_rows(cum, half, forward)
        right = (t & half) != 0
        q_side = right if forward else jnp.logical_not(right)
        w = jnp.exp(jnp.where(q_side, e, -e))
        x = (jnp.where(q_side, q, k) * w).astype(BF16)
        prod = lax.dot_general(x, x, nt, preferred_element_type=F32)
        scores = jnp.where(lev == j, prod, scores)
    diag = jnp.sum(q * k, axis=-1, keepdims=True)
    return scores, diag


def _scan_chunk(q, v, f_raw, lb, st_ref, tri_ref, lev_ref, forward):
    C = SCAN_CHUNK
    f = lb + (1.0 - lb) * _sigmoid(f_raw)
    g = jnp.log(f)
    k = 1.0 - f
    g3 = jnp.concatenate(_split3(g), axis=1)
    cum3 = jnp.dot(tri_ref[...], g3, preferred_element_type=F32)
    cum = cum3[:, :C_DK] + cum3[:, C_DK:2 * C_DK] + cum3[:, 2 * C_DK:]
    last = cum[C - 1:C, :] if forward else cum[0:1, :]
    nt = (((1,), (1,)), ((), ()))
    tn = (((0,), (0,)), ((), ()))
    st = st_ref[...]
    scores, diag = _intra_chunk(q, k, cum, lev_ref[...], forward)
    vb = v.astype(BF16)
    o = (lax.dot_general((q * jnp.exp(cum)).astype(BF16), st.astype(BF16), nt, preferred_element_type=F32)
         + jnp.dot(scores.astype(BF16), vb, preferred_element_type=F32) + diag * v)
    k_dec = (k * jnp.exp(last - cum)).astype(BF16)
    st_ref[...] = st * jnp.exp(last) + lax.dot_general(vb, k_dec, tn, preferred_element_type=F32)
    return o


def _hgrn_scan_kernel(*refs, n_chunks, has_init):
    q_ref, v_ref, ff_ref, fb_ref, lbl_ref, tril_ref, triu_ref, levf_ref, levb_ref = refs[:9]
    refs = refs[9:]
    if has_init:
        s0_ref, refs = refs[0], refs[1:]
    o_ref, refs = refs[0], refs[1:]
    if not has_init:
        sout_ref, refs = refs[0], refs[1:]
    stf_ref, stb_ref = refs
    C = SCAN_CHUNK

    lg = lbl_ref[...]
    ex = jnp.exp(lg - jnp.max(lg, axis=0, keepdims=True))
    prob = ex / jnp.sum(ex, axis=0, keepdims=True)
    lb = jnp.sum(prob[1:DEPTH], axis=0)
    lb_f, lb_b = lb[0:1], lb[1:2]

    if has_init:
        stf_ref[...] = s0_ref[0].T
        stb_ref[...] = s0_ref[1].T
    else:
        stf_ref[...] = jnp.zeros_like(stf_ref)
        stb_ref[...] = jnp.zeros_like(stb_ref)

    def step(c, carry):
        rf = pl.ds(pl.multiple_of(c * C, C), C)
        rb = pl.ds(pl.multiple_of((n_chunks - 1 - c) * C, C), C)
        o_f = _scan_chunk(q_ref[rf, :], v_ref[rf, :], ff_ref[rf, :], lb_f, stf_ref, tril_ref, levf_ref, True)
        o_b = _scan_chunk(q_ref[rb, :], v_ref[rb, :], fb_ref[rb, :], lb_b, stb_ref, triu_ref, levb_ref, False)
        first = 2 * c < n_chunks - 1

        @pl.when(first)
        def _():
            o_ref[rf, :] = o_f
            o_ref[rb, :] = o_b

        @pl.when(jnp.logical_not(first))
        def _():
            if n_chunks % 2 == 1:
                @pl.when(2 * c == n_chunks - 1)
                def _():
                    o_ref[rf, :] = o_f + o_b

                @pl.when(2 * c != n_chunks - 1)
                def _():
                    o_ref[rf, :] += o_f
                    o_ref[rb, :] += o_b
            else:
                o_ref[rf, :] += o_f
                o_ref[rb, :] += o_b
        return carry

    lax.fori_loop(0, n_chunks, step, 0)
    if not has_init:
        sout_ref[0] = stf_ref[...].T
        sout_ref[1] = stb_ref[...].T


def _hgrn_scan(z, lb_logits, seq_len, n_seq, row_off, s0):
    n_chunks = seq_len // SCAN_CHUNK
    blk_off = row_off // seq_len
    heads = C_WIDTH // C_DK
    col = lambda part: pl.BlockSpec((seq_len, C_DK), lambda n, h: (blk_off + n, part * heads + h))
    idx = np.arange(SCAN_CHUNK)
    tril = jnp.asarray(idx[:, None] >= idx[None, :], dtype=BF16)
    triu = jnp.asarray(idx[:, None] <= idx[None, :], dtype=BF16)
    in_specs = [col(0), col(1), col(3), col(4),
                pl.BlockSpec((DEPTH, 2, C_DK), lambda n, h: (0, 0, h)),
                *[pl.BlockSpec((SCAN_CHUNK, SCAN_CHUNK), lambda n, h: (0, 0))] * 4]
    args = [z, z, z, z, lb_logits, tril, triu, _pair_levels(True), _pair_levels(False)]
    o_spec = pl.BlockSpec((seq_len, C_DV), lambda n, h: (n, h))
    o_shape = jax.ShapeDtypeStruct((n_seq * seq_len, C_WIDTH), F32)
    if s0 is not None:
        in_specs.append(pl.BlockSpec((None, 2, None, C_DK, C_DV), lambda n, h: (n, 0, h, 0, 0)))
        args.append(s0)
        out_specs, out_shape = o_spec, o_shape
    else:
        out_specs = [o_spec, pl.BlockSpec((None, 2, None, C_DK, C_DV), lambda n, h: (n, 0, h, 0, 0))]
        out_shape = [o_shape, jax.ShapeDtypeStruct((n_seq, 2, heads, C_DK, C_DV), F32)]
    return pl.pallas_call(
        functools.partial(_hgrn_scan_kernel, n_chunks=n_chunks, has_init=s0 is not None),
        grid=(n_seq, heads),
        in_specs=in_specs, out_specs=out_specs, out_shape=out_shape,
        scratch_shapes=[pltpu.VMEM((C_DV, C_DK), F32), pltpu.VMEM((C_DV, C_DK), F32)],
        compiler_params=_params("arbitrary", "arbitrary"),
        name="hgrn_scan_init" if s0 is not None else "hgrn_scan_zero",
    )(*args)


def _hgrn_out_kernel(x_ref, mod_ref, o_ref, gate_ref, ng_ref, gmat_ref, w_ref, g_ref, b_ref, out_ref):
    m = mod_ref[0]
    o = o_ref[...]
    o = o * lax.rsqrt(_group_mean_square(o, gmat_ref) + EPS) * ng_ref[...] * _silu(gate_ref[...])
    y = jnp.dot(o.astype(BF16), w_ref[...], preferred_element_type=F32)
    out_ref[...] = _residual_norm(x_ref[...], y, m[2:3], g_ref[...], b_ref[...])


def _hgrn_out(x, mod, o, z, norm_g, w_out, g, b):
    row = lambda i: (i, 0)
    return pl.pallas_call(
        _hgrn_out_kernel,
        grid=(N_BLOCKS,),
        in_specs=[pl.BlockSpec((TM, D_MODEL), row),
                  pl.BlockSpec((1, 6, D_MODEL), lambda i: (_mod_row(i), 0, 0)),
                  pl.BlockSpec((TM, C_WIDTH), row), pl.BlockSpec((TM, C_WIDTH), lambda i: (i, 2)),
                  _resident((1, C_WIDTH)), _resident((256, 256)), _resident((C_WIDTH, D_MODEL)),
                  _resident((1, D_MODEL)), _resident((1, D_MODEL))],
        out_specs=pl.BlockSpec((TM, D_MODEL), row),
        out_shape=jax.ShapeDtypeStruct((N_TOK, D_MODEL), F32),
        compiler_params=_params("arbitrary"),
        name="hgrn_out",
    )(x, mod, o, z, jnp.tile(norm_g, C_HEADS)[None], _group_mean_matrix(C_DV), w_out.astype(BF16), g[None], b[None])


def kernel(x_prompt, x_sample, cache_k, cache_v, state_hgrn, c, c_ctx, w_mod, b_mod, ln1_g, ln1_b, ln2_g, ln2_b,
           attn_w_in, attn_q_gain, attn_k_gain, sconv_w, attn_w_out, hgrn_w_in, hgrn_lb_logits, hgrn_norm_g,
           hgrn_w_out, ffn_w_up, ffn_conv_w, ffn_w_down):
    x = jnp.concatenate([x_prompt.reshape(P_TOK, D_MODEL), x_sample.reshape(S_TOK, D_MODEL)], axis=0)
    cond = jnp.concatenate([c_ctx[None], c, jnp.zeros((MOD_ROWS - 1 - DEC_BATCH, D_MODEL), F32)], axis=0)
    mod = _modulation(cond, w_mod, b_mod).reshape(DEPTH, MOD_ROWS, 6, D_MODEL)

    q, k, v, bg, u = _attn_in(x, mod[0], attn_w_in[0], attn_q_gain[0], attn_k_gain[0])
    att_p = _attend_prompt(q, k, v)
    att_s = _attend_latent(q, k, v, cache_k[:, 0].reshape(DEC_BATCH, PAST_LEN, KV_WIDTH),
                           cache_v[:, 0].reshape(DEC_BATCH, PAST_LEN, KV_WIDTH))
    att = jnp.concatenate([att_p, att_s], axis=0)
    x = _attn_out(x, mod[0], att, bg, u, sconv_w[0], attn_w_out[0], ln1_g[0], ln1_b[0])
    x = _ffn(x, mod[0], ffn_w_up[0], ffn_conv_w[0], ffn_w_down[0], ln2_g[0], ln2_b[0])
    new_k = k[:P_TOK].reshape(BATCH, 1, SEQ, N_KV, HEAD_DIM)
    new_v = v[:P_TOK].reshape(BATCH, 1, SEQ, N_KV, HEAD_DIM)

    z = _hgrn_in(x, mod[1], hgrn_w_in[0])
    o_p, s_new = _hgrn_scan(z, hgrn_lb_logits, SEQ, BATCH, 0, None)
    o_s = _hgrn_scan(z, hgrn_lb_logits, DEC_SEQ, DEC_BATCH, P_TOK, state_hgrn[:, 0])
    o = jnp.concatenate([o_p, o_s], axis=0)
    x = _hgrn_out(x, mod[1], o, z, hgrn_norm_g[0], hgrn_w_out[0], ln1_g[1], ln1_b[1])
    x = _ffn(x, mod[1], ffn_w_up[1], ffn_conv_w[1], ffn_w_down[1], ln2_g[1], ln2_b[1])

    y_prompt = x[:P_TOK].reshape(BATCH, SEQ, D_MODEL)
    y_sample = x[P_TOK:].reshape(DEC_BATCH, DEC_SEQ, D_MODEL)
    return (y_prompt, y_sample, new_k, new_v, s_new[:, None])
```

```python
import functools

import jax
import jax.numpy as jnp
import numpy as np
from jax import lax
from jax.experimental import pallas as pl
from jax.experimental.pallas import tpu as pltpu

F32 = jnp.float32
BF16 = jnp.bfloat16

D_MODEL = 1024
BATCH = 32
SEQ = 256
DEPTH = 2
DEC_BATCH = 4
DEC_SEQ = 2048
PAST_LEN = 512
GRID_W = 64
N_HEADS = 8
N_KV = 2
HEAD_DIM = 64
Q_GROUP = N_HEADS // N_KV
ATT_WIDTH = N_HEADS * HEAD_DIM
KV_WIDTH = N_KV * HEAD_DIM
ROPE_AXIS_DIM = HEAD_DIM // 2
ROPE_THETA = 10000.0
SC_WIDTH = D_MODEL - ATT_WIDTH
CONV_WIDTH = 3
C_HEADS = 8
C_DK = D_MODEL // C_HEADS
C_DV = D_MODEL // C_HEADS
C_WIDTH = C_HEADS * C_DK
D_FF = 2816
EVEN_IN_WIDTH = ATT_WIDTH + 2 * KV_WIDTH + 3 * SC_WIDTH
ALPHA = (2 * DEPTH) ** 0.25
EPS = 1e-6

P_TOK = BATCH * SEQ
S_TOK = DEC_BATCH * DEC_SEQ
N_TOK = P_TOK + S_TOK
MOD_ROWS = 8
TM = 512
HALO = 8
FF_CHUNK = 256
SCAN_CHUNK = 128
SCAN_BLOCK = DEC_SEQ
TQ = 256
VMEM_LIMIT = 56 * 1024 * 1024

assert P_TOK % TM == 0 and S_TOK % TM == 0 and DEC_SEQ % TM == 0 and TM % SEQ == 0
assert D_FF % FF_CHUNK == 0 and SEQ % SCAN_CHUNK == 0 and DEC_SEQ % SCAN_CHUNK == 0
assert SCAN_BLOCK % SEQ == 0 and P_TOK % SCAN_BLOCK == 0
P_BLOCKS = P_TOK // TM
N_BLOCKS = N_TOK // TM
S_BLOCKS_PER_SEQ = DEC_SEQ // TM


def _params(*sem):
    return pltpu.CompilerParams(dimension_semantics=sem, vmem_limit_bytes=VMEM_LIMIT)


def _mod_row(i):
    return jnp.where(i < P_BLOCKS, 0, 1 + (i - P_BLOCKS) // S_BLOCKS_PER_SEQ)


def _resident(shape):
    nd = len(shape)
    return pl.BlockSpec(shape, lambda *_: (0,) * nd, pipeline_mode=pl.Buffered(1))


def _sigmoid(x):
    return 1.0 / (1.0 + jnp.exp(-x))


def _silu(x):
    return x * _sigmoid(x)


def _layer_norm(x, g, b):
    mu = jnp.mean(x, axis=-1, keepdims=True)
    xc = x - mu
    var = jnp.mean(xc * xc, axis=-1, keepdims=True)
    return xc * lax.rsqrt(var + EPS) * g + b


def _group_mean_matrix(group):
    idx = np.arange(256)
    return jnp.asarray((idx[:, None] // group == idx[None, :] // group) / group, dtype=BF16)


def _group_mean_square(x, gmat_ref):
    sq = (x * x).astype(BF16)
    n = x.shape[1] // 256
    parts = [jnp.dot(sq[:, 256 * j:256 * (j + 1)], gmat_ref[...], preferred_element_type=F32) for j in range(n)]
    return parts[0] if n == 1 else jnp.concatenate(parts, axis=1)


def _seq_pos(i, rows, first_row):
    seq_len = jnp.where(i < P_BLOCKS, SEQ, DEC_SEQ)
    r = lax.broadcasted_iota(jnp.int32, (rows, 1), 0) + (i * TM + first_row)
    return r & (seq_len - 1), seq_len


def _shifted_rows(u, keep_prev, keep_next):
    n = u.shape[0]
    return pltpu.roll(u, 1, 0) * keep_prev, pltpu.roll(u, n - 1, 0) * keep_next


MOD_TN = 1536


def _mod_kernel(cond_ref, w_ref, b_ref, o_ref):
    s = _silu(cond_ref[...])
    o_ref[...] = jnp.dot(s, w_ref[...], preferred_element_type=F32) + b_ref[...]


def _modulation(cond, w_mod, b_mod):
    n_out = 6 * D_MODEL
    return pl.pallas_call(
        _mod_kernel,
        grid=(DEPTH, n_out // MOD_TN),
        in_specs=[pl.BlockSpec((MOD_ROWS, D_MODEL), lambda l, j: (0, 0)),
                  pl.BlockSpec((None, D_MODEL, MOD_TN), lambda l, j: (l, 0, j)),
                  pl.BlockSpec((None, 1, MOD_TN), lambda l, j: (l, 0, j))],
        out_specs=pl.BlockSpec((None, MOD_ROWS, MOD_TN), lambda l, j: (l, 0, j)),
        out_shape=jax.ShapeDtypeStruct((DEPTH, MOD_ROWS, n_out), F32),
        compiler_params=_params("arbitrary", "arbitrary"),
        name="modulation",
    )(cond, w_mod, b_mod.reshape(DEPTH, 1, n_out))


def _rope_tables():
    t = np.arange(DEC_SEQ)
    half = ROPE_AXIS_DIM // 2
    inv = (ROPE_THETA ** (-np.arange(0, ROPE_AXIS_DIM, 2, dtype=np.float32) / ROPE_AXIS_DIM)).astype(np.float32)
    row = (t // GRID_W).astype(np.float32)[:, None] * inv
    col = (t % GRID_W).astype(np.float32)[:, None] * inv
    ang = np.concatenate([row, row, col, col], axis=1).astype(np.float32)
    sign = np.concatenate([-np.ones(half), np.ones(half)] * 2).astype(np.float32)
    cos = np.tile(np.cos(ang), (1, 128 // HEAD_DIM))
    sin = np.tile(np.sin(ang) * sign, (1, 128 // HEAD_DIM))
    return jnp.asarray(cos, F32), jnp.asarray(sin, F32)


def _rope(x, cos, sin):
    n = x.shape[1] // 128
    if n > 1:
        cos = jnp.concatenate([cos] * n, axis=1)
        sin = jnp.concatenate([sin] * n, axis=1)
    lane = lax.broadcasted_iota(jnp.int32, x.shape, 1)
    half = ROPE_AXIS_DIM // 2
    partner = jnp.where((lane & (ROPE_AXIS_DIM - 1)) < half,
                        pltpu.roll(x, x.shape[1] - half, 1), pltpu.roll(x, half, 1))
    return x * cos + partner * sin


def _attn_in_kernel(x_ref, mod_ref, w_ref, qg_ref, kg_ref, gmat_ref, cos_ref, sin_ref,
                    q_ref, k_ref, v_ref, bg_ref, u_ref):
    i = pl.program_id(0)
    m = mod_ref[0]
    h = (x_ref[...] * (1.0 + m[1:2]) + m[0:1]).astype(BF16)
    z = jnp.dot(h, w_ref[...], preferred_element_type=F32)
    q = z[:, :ATT_WIDTH]
    k = z[:, ATT_WIDTH:ATT_WIDTH + KV_WIDTH]
    o = ATT_WIDTH + 2 * KV_WIDTH
    q = q * lax.rsqrt(_group_mean_square(q, gmat_ref) + EPS) * qg_ref[...]
    kk = (k * k).astype(BF16)
    k_ms = jnp.dot(kk, gmat_ref[:KV_WIDTH, :KV_WIDTH], preferred_element_type=F32)
    k = k * lax.rsqrt(k_ms + EPS) * kg_ref[...]
    v_ref[...] = z[:, ATT_WIDTH + KV_WIDTH:o]
    bg_ref[...] = z[:, o:o + SC_WIDTH].astype(BF16)
    u_ref[...] = z[:, o + SC_WIDTH:o + 2 * SC_WIDTH] * z[:, o + 2 * SC_WIDTH:]

    @pl.when(i < P_BLOCKS)
    def _():
        q_ref[...] = (q * HEAD_DIM ** -0.5).astype(BF16)
        k_ref[...] = k

    @pl.when(i >= P_BLOCKS)
    def _():
        cos, sin = cos_ref[...], sin_ref[...]
        q_ref[...] = (_rope(q, cos, sin) * HEAD_DIM ** -0.5).astype(BF16)
        k_ref[...] = _rope(k, cos, sin)


def _attn_in(x, mod, w_in, q_gain, k_gain):
    cos, sin = _rope_tables()
    row = lambda i: (i, 0)
    rope_row = lambda i: (jnp.maximum(i - P_BLOCKS, 0) % S_BLOCKS_PER_SEQ, 0)
    tok = lambda w, dt: jax.ShapeDtypeStruct((N_TOK, w), dt)
    return pl.pallas_call(
        _attn_in_kernel,
        grid=(N_BLOCKS,),
        in_specs=[pl.BlockSpec((TM, D_MODEL), row),
                  pl.BlockSpec((1, 6, D_MODEL), lambda i: (_mod_row(i), 0, 0)),
                  _resident((D_MODEL, EVEN_IN_WIDTH)),
                  _resident((1, ATT_WIDTH)), _resident((1, KV_WIDTH)), _resident((256, 256)),
                  pl.BlockSpec((TM, 128), rope_row), pl.BlockSpec((TM, 128), rope_row)],
        out_specs=[pl.BlockSpec((TM, ATT_WIDTH), row), pl.BlockSpec((TM, KV_WIDTH), row),
                   pl.BlockSpec((TM, KV_WIDTH), row), pl.BlockSpec((TM, SC_WIDTH), row),
                   pl.BlockSpec((TM, SC_WIDTH), row)],
        out_shape=[tok(ATT_WIDTH, BF16), tok(KV_WIDTH, F32), tok(KV_WIDTH, F32), tok(SC_WIDTH, BF16),
                   tok(SC_WIDTH, F32)],
        compiler_params=_params("arbitrary"),
        name="attn_in",
    )(x, mod, w_in.astype(BF16), jnp.tile(q_gain, N_HEADS)[None], jnp.tile(k_gain, N_KV)[None],
      _group_mean_matrix(HEAD_DIM), cos, sin)


def _attend_kernel(q_ref, *refs, n_kv_sets):
    kv_refs, o_ref = refs[:2 * n_kv_sets], refs[2 * n_kv_sets]
    ks = [r[...].astype(BF16) for r in kv_refs[:n_kv_sets]]
    vs = [r[...].astype(BF16) for r in kv_refs[n_kv_sets:]]
    q = q_ref[...]
    nt = (((1,), (1,)), ((), ()))
    for hd in range(N_HEADS):
        g = hd // Q_GROUP
        qh = q[:, hd * HEAD_DIM:(hd + 1) * HEAD_DIM]
        s = [lax.dot_general(qh, kx[:, g * HEAD_DIM:(g + 1) * HEAD_DIM], nt, preferred_element_type=F32)
             for kx in ks]
        mx = functools.reduce(jnp.maximum, [jnp.max(sx, axis=-1, keepdims=True) for sx in s])
        p = [jnp.exp(sx - mx) for sx in s]
        den = functools.reduce(jnp.add, [jnp.sum(px, axis=-1, keepdims=True) for px in p])
        acc = functools.reduce(jnp.add, [
            jnp.dot(px.astype(BF16), vx[:, g * HEAD_DIM:(g + 1) * HEAD_DIM], preferred_element_type=F32)
            for px, vx in zip(p, vs)])
        o_ref[:, hd * HEAD_DIM:(hd + 1) * HEAD_DIM] = (acc / den).astype(BF16)


def _attend_prompt(q, k, v):
    blk = lambda w: pl.BlockSpec((SEQ, w), lambda b: (b, 0))
    return pl.pallas_call(
        functools.partial(_attend_kernel, n_kv_sets=1),
        grid=(BATCH,),
        in_specs=[blk(ATT_WIDTH), blk(KV_WIDTH), blk(KV_WIDTH)],
        out_specs=blk(ATT_WIDTH),
        out_shape=jax.ShapeDtypeStruct((P_TOK, ATT_WIDTH), BF16),
        compiler_params=_params("arbitrary"),
        name="attend_prompt",
    )(q, k, v)


def _attend_latent(q, k, v, ctx_k, ctx_v):
    nq = DEC_SEQ // TQ
    q_off, kv_off = P_TOK // TQ, P_TOK // DEC_SEQ
    lat = pl.BlockSpec((DEC_SEQ, KV_WIDTH), lambda b, j: (kv_off + b, 0))
    ctx = pl.BlockSpec((None, PAST_LEN, KV_WIDTH), lambda b, j: (b, 0, 0))
    return pl.pallas_call(
        functools.partial(_attend_kernel, n_kv_sets=2),
        grid=(DEC_BATCH, nq),
        in_specs=[pl.BlockSpec((TQ, ATT_WIDTH), lambda b, j: (q_off + b * nq + j, 0)), lat, ctx, lat, ctx],
        out_specs=pl.BlockSpec((TQ, ATT_WIDTH), lambda b, j: (b * nq + j, 0)),
        out_shape=jax.ShapeDtypeStruct((S_TOK, ATT_WIDTH), BF16),
        compiler_params=_params("arbitrary", "arbitrary"),
        name="attend_latent",
    )(q, k, ctx_k, v, ctx_v)


def _residual_norm(x, y, gate, g, b):
    return _layer_norm(ALPHA * x + gate * y, g, b)


def _halo_specs(width):
    per = TM // HALO
    prev = pl.BlockSpec((HALO, width), lambda i: (jnp.maximum(i * per - 1, 0), 0))
    nxt = pl.BlockSpec((HALO, width), lambda i: (jnp.minimum((i + 1) * per, N_TOK // HALO - 1), 0))
    return prev, nxt


def _attn_out_kernel(x_ref, mod_ref, att_ref, bg_ref, u_ref, up_ref, un_ref, cw_ref, w_ref, g_ref, b_ref, o_ref):
    i = pl.program_id(0)
    m = mod_ref[0]
    pos, seq_len = _seq_pos(i, TM, 0)
    r = lax.broadcasted_iota(jnp.int32, (TM, 1), 0)
    u = u_ref[...]
    u_prev = jnp.where(r == 0, up_ref[HALO - 1:HALO, :], pltpu.roll(u, 1, 0))
    u_next = jnp.where(r == TM - 1, un_ref[0:1, :], pltpu.roll(u, TM - 1, 0))
    u_prev = jnp.where(pos == 0, 0.0, u_prev)
    u_next = jnp.where(pos == seq_len - 1, 0.0, u_next)
    cw = cw_ref[...]
    sc = bg_ref[...].astype(F32) * (cw[0:1] * u_prev + cw[1:2] * u + cw[2:3] * u_next)
    y = (jnp.dot(att_ref[...], w_ref[:ATT_WIDTH, :], preferred_element_type=F32)
         + jnp.dot(sc.astype(BF16), w_ref[ATT_WIDTH:, :], preferred_element_type=F32))
    o_ref[...] = _residual_norm(x_ref[...], y, m[2:3], g_ref[...], b_ref[...])


def _attn_out(x, mod, att, bg, u, conv_w, w_out, g, b):
    row = lambda i: (i, 0)
    up, un = _halo_specs(SC_WIDTH)
    return pl.pallas_call(
        _attn_out_kernel,
        grid=(N_BLOCKS,),
        in_specs=[pl.BlockSpec((TM, D_MODEL), row),
                  pl.BlockSpec((1, 6, D_MODEL), lambda i: (_mod_row(i), 0, 0)),
                  pl.BlockSpec((TM, ATT_WIDTH), row), pl.BlockSpec((TM, SC_WIDTH), row),
                  pl.BlockSpec((TM, SC_WIDTH), row), up, un,
                  _resident((CONV_WIDTH, SC_WIDTH)), _resident((D_MODEL, D_MODEL)),
                  _resident((1, D_MODEL)), _resident((1, D_MODEL))],
        out_specs=pl.BlockSpec((TM, D_MODEL), row),
        out_shape=jax.ShapeDtypeStruct((N_TOK, D_MODEL), F32),
        compiler_params=_params("arbitrary"),
        name="attn_out",
    )(x, mod, att, bg, u, u, u, conv_w, w_out.astype(BF16), g[None], b[None])


def _ffn_kernel(x_ref, xp_ref, xn_ref, mod_ref, wu_ref, cw_ref, wd_ref, g_ref, b_ref, o_ref, h_ref, acc_ref):
    i = pl.program_id(0)
    m = mod_ref[0]
    rows = TM + 2 * HALO
    scale, shift = 1.0 + m[4:5], m[3:4]
    h_ref[:HALO, :] = (xp_ref[...] * scale + shift).astype(BF16)
    h_ref[HALO:HALO + TM, :] = (x_ref[...] * scale + shift).astype(BF16)
    h_ref[HALO + TM:, :] = (xn_ref[...] * scale + shift).astype(BF16)
    pos, seq_len = _seq_pos(i, rows, -HALO)
    keep_prev = (pos != 0).astype(F32)
    keep_next = (pos != seq_len - 1).astype(F32)
    acc_ref[...] = jnp.zeros_like(acc_ref)

    def conv(u, col):
        w = cw_ref[:, pl.ds(col, FF_CHUNK)]
        u_prev, u_next = _shifted_rows(u, keep_prev, keep_next)
        return (w[0:1] * u_prev + w[1:2] * u + w[2:3] * u_next)[HALO:HALO + TM]

    def step(c, carry):
        col_a = pl.multiple_of(c * FF_CHUNK, FF_CHUNK)
        col_g = pl.multiple_of(D_FF + c * FF_CHUNK, FF_CHUNK)
        h = h_ref[...]
        a = conv(jnp.dot(h, wu_ref[:, pl.ds(col_a, FF_CHUNK)], preferred_element_type=F32), col_a)
        gt = conv(jnp.dot(h, wu_ref[:, pl.ds(col_g, FF_CHUNK)], preferred_element_type=F32), col_g)
        act = (_silu(gt) * a).astype(BF16)
        acc_ref[...] += jnp.dot(act, wd_ref[pl.ds(col_a, FF_CHUNK), :], preferred_element_type=F32)
        return carry

    lax.fori_loop(0, D_FF // FF_CHUNK, step, 0)
    o_ref[...] = _residual_norm(x_ref[...], acc_ref[...], m[5:6], g_ref[...], b_ref[...])


def _ffn(x, mod, w_up, conv_w, w_down, g, b):
    row = lambda i: (i, 0)
    xp, xn = _halo_specs(D_MODEL)
    return pl.pallas_call(
        _ffn_kernel,
        grid=(N_BLOCKS,),
        in_specs=[pl.BlockSpec((TM, D_MODEL), row), xp, xn,
                  pl.BlockSpec((1, 6, D_MODEL), lambda i: (_mod_row(i), 0, 0)),
                  _resident((D_MODEL, 2 * D_FF)), _resident((CONV_WIDTH, 2 * D_FF)), _resident((D_FF, D_MODEL)),
                  _resident((1, D_MODEL)), _resident((1, D_MODEL))],
        out_specs=pl.BlockSpec((TM, D_MODEL), row),
        out_shape=jax.ShapeDtypeStruct((N_TOK, D_MODEL), F32),
        scratch_shapes=[pltpu.VMEM((TM + 2 * HALO, D_MODEL), BF16), pltpu.VMEM((TM, D_MODEL), F32)],
        compiler_params=_params("arbitrary"),
        name="conv_ffn",
    )(x, x, x, mod, w_up.astype(BF16), conv_w, w_down.astype(BF16), g[None], b[None])


def _hgrn_in_kernel(x_ref, mod_ref, w_ref, z_ref):
    m = mod_ref[0]
    h = (x_ref[...] * (1.0 + m[1:2]) + m[0:1]).astype(BF16)
    z_ref[...] = jnp.dot(h, w_ref[...], preferred_element_type=F32)


def _hgrn_in(x, mod, w_in):
    n_col = 5 * C_WIDTH // C_WIDTH
    return pl.pallas_call(
        _hgrn_in_kernel,
        grid=(N_BLOCKS, n_col),
        in_specs=[pl.BlockSpec((TM, D_MODEL), lambda i, j: (i, 0)),
                  pl.BlockSpec((1, 6, D_MODEL), lambda i, j: (_mod_row(i), 0, 0)),
                  pl.BlockSpec((D_MODEL, C_WIDTH), lambda i, j: (0, j))],
        out_specs=pl.BlockSpec((TM, C_WIDTH), lambda i, j: (i, j)),
        out_shape=jax.ShapeDtypeStruct((N_TOK, 5 * C_WIDTH), F32),
        compiler_params=_params("arbitrary", "arbitrary"),
        name="hgrn_in",
    )(x, mod, w_in.astype(BF16))


SCAN_LEVELS = tuple(1 << j for j in range(SCAN_CHUNK.bit_length() - 1))


def _split3(x):
    a = x.astype(BF16)
    r = x - a.astype(F32)
    b = r.astype(BF16)
    c = (r - b.astype(F32)).astype(BF16)
    return a, b, c


def _ref_rows(b, half, forward):
    C = SCAN_CHUNK
    off = half - 1 if forward else half
    b3 = b.reshape(C // 8, 8, C_DK)
    sub = lax.broadcasted_iota(jnp.int32, b3.shape, 1)
    blk = 2 * half
    out = None
    for j in range(8 // blk):
        cand = jnp.broadcast_to(b3[:, j * blk + off:j * blk + off + 1, :], b3.shape)
        out = cand if out is None else jnp.where(sub >= j * blk, cand, out)
    return out.reshape(C, C_DK)


N_LEVELS = len(SCAN_LEVELS)
DIAG_CODE = 2 * N_LEVELS


def _pair_codes():
    idx = np.arange(SCAN_CHUNK)
    t, s = idx[:, None], idx[None, :]
    lev = np.floor(np.log2(np.maximum(t ^ s, 1))).astype(np.int32)
    code = np.where(t > s, lev, np.where(t < s, N_LEVELS + lev, DIAG_CODE))
    return jnp.asarray(code, dtype=jnp.int32)


NT_DIMS = (((1,), (1,)), ((), ()))
TN_DIMS = (((0,), (0,)), ((), ()))


def _fine_levels(q, k, cum, code, forward, scores):
    C = SCAN_CHUNK
    t = lax.broadcasted_iota(jnp.int32, (C, 1), 0)
    nt = NT_DIMS
    base = 0 if forward else N_LEVELS
    for j, half in enumerate(SCAN_LEVELS):
        if half >= 8:
            continue
        e = cum - _ref_rows(cum, half, forward)
        right = (t & half) != 0
        q_side = right if forward else jnp.logical_not(right)
        w = jnp.exp2(jnp.where(q_side, e, -e))
        x = (jnp.where(q_side, q, k) * w).astype(BF16)
        prod = lax.dot_general(x, x, nt, preferred_element_type=F32)
        scores = jnp.where(code == base + j, prod, scores)
    return scores


def _coarse_levels(q, k, cum, code, forward, rows):
    C = SCAN_CHUNK
    nt = NT_DIMS
    base = 0 if forward else N_LEVELS
    for j, half in enumerate(SCAN_LEVELS):
        if half < 8:
            continue
        blk = 2 * half
        xq, xk, q_rows = [], [], []
        zero = jnp.zeros((half, C_DK), F32)
        for n in range(C // blk):
            lo, hi = slice(n * blk, n * blk + half), slice(n * blk + half, (n + 1) * blk)
            if forward:
                ref = cum[n * blk + half - 1:n * blk + half]
                xk += [k[lo] * jnp.exp2(ref - cum[lo]), zero]
                xq.append(q[hi] * jnp.exp2(cum[hi] - ref))
                q_rows.append(n * blk + half)
            else:
                ref = cum[n * blk + half:n * blk + half + 1]
                xq.append(q[lo] * jnp.exp2(cum[lo] - ref))
                xk += [zero, k[hi] * jnp.exp2(ref - cum[hi])]
                q_rows.append(n * blk)
        prod = lax.dot_general(jnp.concatenate(xq, axis=0).astype(BF16), jnp.concatenate(xk, axis=0).astype(BF16),
                               nt, preferred_element_type=F32)
        for n, r0 in enumerate(q_rows):
            for i in range(half // 8):
                rb, src = r0 // 8 + i, n * half + 8 * i
                rows[rb] = jnp.where(code[8 * rb:8 * rb + 8] == base + j, prod[src:src + 8], rows[rb])
    return rows


def _decay_stage(f_raw, lb, tri_ref):
    f = lb + (1.0 - lb) * _sigmoid(f_raw)
    g3 = jnp.concatenate(_split3(jnp.log2(f)), axis=1)
    cum3 = jnp.dot(tri_ref[...], g3, preferred_element_type=F32)
    return cum3[:, :C_DK] + cum3[:, C_DK:2 * C_DK] + cum3[:, 2 * C_DK:], 1.0 - f


def _hgrn_scan_kernel(*refs, sub_len, has_init):
    q_ref, v_ref, ff_ref, fb_ref, lbl_ref, tril_ref, triu_ref, code_ref = refs[:8]
    refs = refs[8:]
    if has_init:
        s0_ref, refs = refs[0], refs[1:]
    o_ref, refs = refs[0], refs[1:]
    if not has_init:
        sout_ref, refs = refs[0], refs[1:]
    stf_ref, stb_ref, qdf_ref, qdb_ref, kdf_ref, kdb_ref, decf_ref, decb_ref, vt_ref = refs
    C = SCAN_CHUNK
    n_chunks = SCAN_BLOCK // C
    sub_chunks = sub_len // C

    lg = lbl_ref[...]
    ex = jnp.exp(lg - jnp.max(lg, axis=0, keepdims=True))
    prob = ex / jnp.sum(ex, axis=0, keepdims=True)
    lb = jnp.sum(prob[1:DEPTH], axis=0)
    lb_f, lb_b = lb[0:1], lb[1:2]

    code = code_ref[...]
    zeros = jnp.zeros((C, C_DK), F32)

    def local_step(i, carry):
        (cum_f, k_f, cum_b, k_b), scores_b16 = carry
        c3 = jnp.clip(i - 2, 0, n_chunks - 1)
        r3 = pl.ds(pl.multiple_of(c3 * C, C), C)
        v3 = v_ref[r3, :]
        o_ref[r3, :] = jnp.dot(scores_b16, v3.astype(BF16), preferred_element_type=F32)
        vt_ref[c3] = v3.T.astype(BF16)
        c2 = jnp.clip(i - 1, 0, n_chunks - 1)
        r2 = pl.ds(pl.multiple_of(c2 * C, C), C)
        q = q_ref[r2, :]
        scores = _fine_levels(q, k_f, cum_f, code, True, jnp.zeros((C, C), F32))
        scores = _fine_levels(q, k_b, cum_b, code, False, scores)
        rows = [scores[8 * n:8 * n + 8] for n in range(C // 8)]
        rows = _coarse_levels(q, k_f, cum_f, code, True, rows)
        rows = _coarse_levels(q, k_b, cum_b, code, False, rows)
        same_row = jnp.sum(q * (k_f + k_b), axis=-1, keepdims=True)
        scores = jnp.where(code == DIAG_CODE, same_row, jnp.concatenate(rows, axis=0))
        last_f, last_b = cum_f[C - 1:C, :], cum_b[0:1, :]
        qdf_ref[r2, :] = (q * jnp.exp2(cum_f)).astype(BF16)
        qdb_ref[r2, :] = (q * jnp.exp2(cum_b)).astype(BF16)
        kdf_ref[r2, :] = (k_f * jnp.exp2(last_f - cum_f)).astype(BF16)
        kdb_ref[r2, :] = (k_b * jnp.exp2(last_b - cum_b)).astype(BF16)
        decf_ref[pl.ds(c2, 1), :] = jnp.exp2(last_f)
        decb_ref[pl.ds(c2, 1), :] = jnp.exp2(last_b)
        c1 = jnp.minimum(i, n_chunks - 1)
        r1 = pl.ds(pl.multiple_of(c1 * C, C), C)
        nxt = _decay_stage(ff_ref[r1, :], lb_f, tril_ref) + _decay_stage(fb_ref[r1, :], lb_b, triu_ref)
        return nxt, scores.astype(BF16)

    lax.fori_loop(0, n_chunks + 2, local_step, ((zeros, zeros, zeros, zeros), jnp.zeros((C, C), BF16)))

    for sub in range(SCAN_BLOCK // sub_len):
        first = sub * sub_chunks
        if has_init:
            stf_ref[...] = s0_ref[0].T
            stb_ref[...] = s0_ref[1].T
        else:
            stf_ref[...] = jnp.zeros_like(stf_ref)
            stb_ref[...] = jnp.zeros_like(stb_ref)

        def state_step(c, carry):
            for cc, st_ref, qd_ref, kd_ref, dec_ref in ((first + c, stf_ref, qdf_ref, kdf_ref, decf_ref),
                                                        (first + sub_chunks - 1 - c, stb_ref, qdb_ref, kdb_ref, decb_ref)):
                r = pl.ds(pl.multiple_of(cc * C, C), C)
                st = st_ref[...]
                o_ref[r, :] += lax.dot_general(qd_ref[r, :], st.astype(BF16), NT_DIMS, preferred_element_type=F32)
                st_ref[...] = st * dec_ref[pl.ds(cc, 1), :] + jnp.dot(vt_ref[cc], kd_ref[r, :],
                                                                      preferred_element_type=F32)
            return carry

        lax.fori_loop(0, sub_chunks, state_step, 0, unroll=2)
        if not has_init:
            sout_ref[sub, 0] = stf_ref[...].T
            sout_ref[sub, 1] = stb_ref[...].T


def _hgrn_scan(z, lb_logits, sub_len, row_off, n_rows, s0):
    n_blocks = n_rows // SCAN_BLOCK
    blk_off = row_off // SCAN_BLOCK
    n_sub = SCAN_BLOCK // sub_len
    heads = C_WIDTH // C_DK
    col = lambda part: pl.BlockSpec((SCAN_BLOCK, C_DK), lambda n, h: (blk_off + n, part * heads + h))
    idx = np.arange(SCAN_CHUNK)
    tril = jnp.asarray(idx[:, None] >= idx[None, :], dtype=BF16)
    triu = jnp.asarray(idx[:, None] <= idx[None, :], dtype=BF16)
    in_specs = [col(0), col(1), col(3), col(4),
                pl.BlockSpec((DEPTH, 2, C_DK), lambda n, h: (0, 0, h)),
                *[pl.BlockSpec((SCAN_CHUNK, SCAN_CHUNK), lambda n, h: (0, 0))] * 3]
    args = [z, z, z, z, lb_logits, tril, triu, _pair_codes()]
    o_spec = pl.BlockSpec((SCAN_BLOCK, C_DV), lambda n, h: (n, h))
    o_shape = jax.ShapeDtypeStruct((n_rows, C_WIDTH), F32)
    if s0 is not None:
        assert n_sub == 1
        in_specs.append(pl.BlockSpec((None, 2, None, C_DK, C_DV), lambda n, h: (n, 0, h, 0, 0)))
        args.append(s0)
        out_specs, out_shape = o_spec, o_shape
    else:
        out_specs = [o_spec, pl.BlockSpec((n_sub, 2, None, C_DK, C_DV), lambda n, h: (n, 0, h, 0, 0))]
        out_shape = [o_shape, jax.ShapeDtypeStruct((n_blocks * n_sub, 2, heads, C_DK, C_DV), F32)]
    n_chunks = SCAN_BLOCK // SCAN_CHUNK
    return pl.pallas_call(
        functools.partial(_hgrn_scan_kernel, sub_len=sub_len, has_init=s0 is not None),
        grid=(n_blocks, heads),
        in_specs=in_specs, out_specs=out_specs, out_shape=out_shape,
        scratch_shapes=[pltpu.VMEM((C_DV, C_DK), F32)] * 2 + [pltpu.VMEM((SCAN_BLOCK, C_DK), BF16)] * 4
        + [pltpu.VMEM((n_chunks, C_DK), F32)] * 2 + [pltpu.VMEM((n_chunks, C_DV, SCAN_CHUNK), BF16)],
        compiler_params=_params("arbitrary", "arbitrary"),
        name="hgrn_scan_init" if s0 is not None else "hgrn_scan_zero",
    )(*args)


def _hgrn_out_kernel(x_ref, mod_ref, o_ref, gate_ref, ng_ref, gmat_ref, w_ref, g_ref, b_ref, out_ref):
    m = mod_ref[0]
    o = o_ref[...]
    o = o * lax.rsqrt(_group_mean_square(o, gmat_ref) + EPS) * ng_ref[...] * _silu(gate_ref[...])
    y = jnp.dot(o.astype(BF16), w_ref[...], preferred_element_type=F32)
    out_ref[...] = _residual_norm(x_ref[...], y, m[2:3], g_ref[...], b_ref[...])


def _hgrn_out(x, mod, o, z, norm_g, w_out, g, b):
    row = lambda i: (i, 0)
    return pl.pallas_call(
        _hgrn_out_kernel,
        grid=(N_BLOCKS,),
        in_specs=[pl.BlockSpec((TM, D_MODEL), row),
                  pl.BlockSpec((1, 6, D_MODEL), lambda i: (_mod_row(i), 0, 0)),
                  pl.BlockSpec((TM, C_WIDTH), row), pl.BlockSpec((TM, C_WIDTH), lambda i: (i, 2)),
                  _resident((1, C_WIDTH)), _resident((256, 256)), _resident((C_WIDTH, D_MODEL)),
                  _resident((1, D_MODEL)), _resident((1, D_MODEL))],
        out_specs=pl.BlockSpec((TM, D_MODEL), row),
        out_shape=jax.ShapeDtypeStruct((N_TOK, D_MODEL), F32),
        compiler_params=_params("arbitrary"),
        name="hgrn_out",
    )(x, mod, o, z, jnp.tile(norm_g, C_HEADS)[None], _group_mean_matrix(C_DV), w_out.astype(BF16), g[None], b[None])


def kernel(x_prompt, x_sample, cache_k, cache_v, state_hgrn, c, c_ctx, w_mod, b_mod, ln1_g, ln1_b, ln2_g, ln2_b,
           attn_w_in, attn_q_gain, attn_k_gain, sconv_w, attn_w_out, hgrn_w_in, hgrn_lb_logits, hgrn_norm_g,
           hgrn_w_out, ffn_w_up, ffn_conv_w, ffn_w_down):
    x = jnp.concatenate([x_prompt.reshape(P_TOK, D_MODEL), x_sample.reshape(S_TOK, D_MODEL)], axis=0)
    cond = jnp.concatenate([c_ctx[None], c, jnp.zeros((MOD_ROWS - 1 - DEC_BATCH, D_MODEL), F32)], axis=0)
    mod = _modulation(cond, w_mod, b_mod).reshape(DEPTH, MOD_ROWS, 6, D_MODEL)

    q, k, v, bg, u = _attn_in(x, mod[0], attn_w_in[0], attn_q_gain[0], attn_k_gain[0])
    att_p = _attend_prompt(q, k, v)
    att_s = _attend_latent(q, k, v, cache_k[:, 0].reshape(DEC_BATCH, PAST_LEN, KV_WIDTH),
                           cache_v[:, 0].reshape(DEC_BATCH, PAST_LEN, KV_WIDTH))
    att = jnp.concatenate([att_p, att_s], axis=0)
    x = _attn_out(x, mod[0], att, bg, u, sconv_w[0], attn_w_out[0], ln1_g[0], ln1_b[0])
    x = _ffn(x, mod[0], ffn_w_up[0], ffn_conv_w[0], ffn_w_down[0], ln2_g[0], ln2_b[0])
    new_k = k[:P_TOK].reshape(BATCH, 1, SEQ, N_KV, HEAD_DIM)
    new_v = v[:P_TOK].reshape(BATCH, 1, SEQ, N_KV, HEAD_DIM)

    z = _hgrn_in(x, mod[1], hgrn_w_in[0])
    o_p, s_new = _hgrn_scan(z, hgrn_lb_logits, SEQ, 0, P_TOK, None)
    o_s = _hgrn_scan(z, hgrn_lb_logits, DEC_SEQ, P_TOK, S_TOK, state_hgrn[:, 0])
    o = jnp.concatenate([o_p, o_s], axis=0)
    x = _hgrn_out(x, mod[1], o, z, hgrn_norm_g[0], hgrn_w_out[0], ln1_g[1], ln1_b[1])
    x = _ffn(x, mod[1], ffn_w_up[1], ffn_conv_w[1], ffn_w_down[1], ln2_g[1], ln2_b[1])

    y_prompt = x[:P_TOK].reshape(BATCH, SEQ, D_MODEL)
    y_sample = x[P_TOK:].reshape(DEC_BATCH, DEC_SEQ, D_MODEL)
    return (y_prompt, y_sample, new_k, new_v, s_new[:, None])
```

```python
import functools

import jax
import jax.numpy as jnp
import numpy as np
from jax import lax
from jax.experimental import pallas as pl
from jax.experimental.pallas import tpu as pltpu

F32 = jnp.float32
BF16 = jnp.bfloat16

D_MODEL = 1024
BATCH = 32
SEQ = 256
DEPTH = 2
DEC_BATCH = 4
DEC_SEQ = 2048
PAST_LEN = 512
GRID_W = 64
N_HEADS = 8
N_KV = 2
HEAD_DIM = 64
Q_GROUP = N_HEADS // N_KV
ATT_WIDTH = N_HEADS * HEAD_DIM
KV_WIDTH = N_KV * HEAD_DIM
ROPE_AXIS_DIM = HEAD_DIM // 2
ROPE_THETA = 10000.0
SC_WIDTH = D_MODEL - ATT_WIDTH
CONV_WIDTH = 3
C_HEADS = 8
C_DK = D_MODEL // C_HEADS
C_DV = D_MODEL // C_HEADS
C_WIDTH = C_HEADS * C_DK
D_FF = 2816
EVEN_IN_WIDTH = ATT_WIDTH + 2 * KV_WIDTH + 3 * SC_WIDTH
ALPHA = (2 * DEPTH) ** 0.25
EPS = 1e-6

P_TOK = BATCH * SEQ
S_TOK = DEC_BATCH * DEC_SEQ
N_TOK = P_TOK + S_TOK
MOD_ROWS = 8
SUBLANES = 8
TM = 512
HALO = SUBLANES
FF_CHUNK = 256
SCAN_CHUNK = 128
SCAN_BLOCK = DEC_SEQ
TQ = 256
VMEM_LIMIT = 56 * 1024 * 1024

assert P_TOK % TM == 0 and S_TOK % TM == 0 and DEC_SEQ % TM == 0 and TM % SEQ == 0 and DEC_SEQ % SEQ == 0
assert D_FF % FF_CHUNK == 0 and SEQ % SCAN_CHUNK == 0 and DEC_SEQ % SCAN_CHUNK == 0
assert SCAN_BLOCK % SEQ == 0 and P_TOK % SCAN_BLOCK == 0
P_BLOCKS = P_TOK // TM
N_BLOCKS = N_TOK // TM
S_BLOCKS_PER_SEQ = DEC_SEQ // TM

NT_DIMS = (((1,), (1,)), ((), ()))
TN_DIMS = (((0,), (0,)), ((), ()))


def _params(*sem):
    return pltpu.CompilerParams(dimension_semantics=sem, vmem_limit_bytes=VMEM_LIMIT)


def _mod_row(i):
    return jnp.where(i < P_BLOCKS, 0, 1 + (i - P_BLOCKS) // S_BLOCKS_PER_SEQ)


def _mod_spec():
    return pl.BlockSpec((1, 6, D_MODEL), lambda i: (_mod_row(i), 0, 0))


def _resident(shape):
    nd = len(shape)
    return pl.BlockSpec(shape, lambda *_: (0,) * nd, pipeline_mode=pl.Buffered(1))


def _split_specs(width):
    return (pl.BlockSpec((TM, width), lambda i: (jnp.minimum(i, P_BLOCKS - 1), 0)),
            pl.BlockSpec((TM, width), lambda i: (jnp.maximum(i - P_BLOCKS, 0), 0)))


def _sigmoid(x):
    return 0.5 * jnp.tanh(0.5 * x) + 0.5


def _silu(x):
    hx = 0.5 * x
    return hx * jnp.tanh(hx) + hx


def _layer_norm(x, g, b):
    mu = jnp.mean(x, axis=-1, keepdims=True)
    xc = x - mu
    var = jnp.mean(xc * xc, axis=-1, keepdims=True)
    return xc * lax.rsqrt(var + EPS) * g + b


def _residual_norm(x, y, gate, g, b):
    return _layer_norm(ALPHA * x + gate * y, g, b)


def _group_mean_matrix(group):
    idx = np.arange(256)
    return jnp.asarray((idx[:, None] // group == idx[None, :] // group) / group, dtype=BF16)


def _group_mean_square(x, gmat_ref):
    sq = (x * x).astype(BF16)
    n = x.shape[1] // 256
    parts = [jnp.dot(sq[:, 256 * j:256 * (j + 1)], gmat_ref[...], preferred_element_type=F32) for j in range(n)]
    return parts[0] if n == 1 else jnp.concatenate(parts, axis=1)


def _seq_pos(i, rows, first_row):
    seq_len = jnp.where(i < P_BLOCKS, SEQ, DEC_SEQ)
    r = lax.broadcasted_iota(jnp.int32, (rows, 1), 0) + (i * TM + first_row)
    return r & (seq_len - 1), seq_len


def _scale_tiles(x, factor, tiles):
    parts, pos = [], 0
    for t in sorted(tiles):
        lo = t * SUBLANES
        if lo > pos:
            parts.append(x[pos:lo])
        parts.append(x[lo:lo + SUBLANES] * factor[lo:lo + SUBLANES])
        pos = lo + SUBLANES
    if pos < x.shape[0]:
        parts.append(x[pos:])
    return jnp.concatenate(parts, axis=0)


MOD_TN = 1536


def _mod_kernel(cond_ref, w_ref, b_ref, o_ref):
    s = _silu(cond_ref[...])
    o_ref[...] = jnp.dot(s, w_ref[...], preferred_element_type=F32) + b_ref[...]


def _modulation(cond, w_mod, b_mod):
    n_out = 6 * D_MODEL
    return pl.pallas_call(
        _mod_kernel,
        grid=(DEPTH, n_out // MOD_TN),
        in_specs=[pl.BlockSpec((MOD_ROWS, D_MODEL), lambda l, j: (0, 0)),
                  pl.BlockSpec((None, D_MODEL, MOD_TN), lambda l, j: (l, 0, j)),
                  pl.BlockSpec((None, 1, MOD_TN), lambda l, j: (l, 0, j))],
        out_specs=pl.BlockSpec((None, MOD_ROWS, MOD_TN), lambda l, j: (l, 0, j)),
        out_shape=jax.ShapeDtypeStruct((DEPTH, MOD_ROWS, n_out), F32),
        compiler_params=_params("arbitrary", "arbitrary"),
        name="modulation",
    )(cond, w_mod, b_mod.reshape(DEPTH, 1, n_out))


def _rope_tables():
    t = np.arange(DEC_SEQ)
    half = ROPE_AXIS_DIM // 2
    inv = (ROPE_THETA ** (-np.arange(0, ROPE_AXIS_DIM, 2, dtype=np.float32) / ROPE_AXIS_DIM)).astype(np.float32)
    row = (t // GRID_W).astype(np.float32)[:, None] * inv
    col = (t % GRID_W).astype(np.float32)[:, None] * inv
    ang = np.concatenate([row, row, col, col], axis=1).astype(np.float32)
    sign = np.concatenate([-np.ones(half), np.ones(half)] * 2).astype(np.float32)
    cos = np.tile(np.cos(ang), (1, 128 // HEAD_DIM))
    sin = np.tile(np.sin(ang) * sign, (1, 128 // HEAD_DIM))
    return jnp.asarray(cos, F32), jnp.asarray(sin, F32)


def _rope(x, cos, sin):
    n = x.shape[1] // 128
    if n > 1:
        cos = jnp.concatenate([cos] * n, axis=1)
        sin = jnp.concatenate([sin] * n, axis=1)
    lane = lax.broadcasted_iota(jnp.int32, x.shape, 1)
    half = ROPE_AXIS_DIM // 2
    partner = jnp.where((lane & (ROPE_AXIS_DIM - 1)) < half,
                        pltpu.roll(x, x.shape[1] - half, 1), pltpu.roll(x, half, 1))
    return x * cos + partner * sin


def _attn_in_kernel(xp_ref, xs_ref, mod_ref, w_ref, qg_ref, kg_ref, gmat_ref, cos_ref, sin_ref,
                    q_ref, k_ref, v_ref, bg_ref, u_ref):
    i = pl.program_id(0)
    m = mod_ref[0]
    x = jnp.where(i < P_BLOCKS, xp_ref[...], xs_ref[...])
    h = (x * (1.0 + m[1:2]) + m[0:1]).astype(BF16)
    z = jnp.dot(h, w_ref[...], preferred_element_type=F32)
    q = z[:, :ATT_WIDTH]
    k = z[:, ATT_WIDTH:ATT_WIDTH + KV_WIDTH]
    o = ATT_WIDTH + 2 * KV_WIDTH
    q = q * lax.rsqrt(_group_mean_square(q, gmat_ref) + EPS) * qg_ref[...]
    kk = (k * k).astype(BF16)
    k_ms = jnp.dot(kk, gmat_ref[:KV_WIDTH, :KV_WIDTH], preferred_element_type=F32)
    k = k * lax.rsqrt(k_ms + EPS) * kg_ref[...]
    v_ref[...] = z[:, ATT_WIDTH + KV_WIDTH:o]
    bg_ref[...] = z[:, o:o + SC_WIDTH].astype(BF16)
    u_ref[...] = z[:, o + SC_WIDTH:o + 2 * SC_WIDTH] * z[:, o + 2 * SC_WIDTH:]

    @pl.when(i < P_BLOCKS)
    def _():
        q_ref[...] = (q * HEAD_DIM ** -0.5).astype(BF16)
        k_ref[...] = k

    @pl.when(i >= P_BLOCKS)
    def _():
        cos, sin = cos_ref[...], sin_ref[...]
        q_ref[...] = (_rope(q, cos, sin) * HEAD_DIM ** -0.5).astype(BF16)
        k_ref[...] = _rope(k, cos, sin)


def _attn_in(x_p, x_s, mod, w_in, q_gain, k_gain):
    cos, sin = _rope_tables()
    row = lambda i: (i, 0)
    rope_row = lambda i: (jnp.maximum(i - P_BLOCKS, 0) % S_BLOCKS_PER_SEQ, 0)
    tok = lambda w, dt: jax.ShapeDtypeStruct((N_TOK, w), dt)
    return pl.pallas_call(
        _attn_in_kernel,
        grid=(N_BLOCKS,),
        in_specs=[*_split_specs(D_MODEL), _mod_spec(),
                  _resident((D_MODEL, EVEN_IN_WIDTH)),
                  _resident((1, ATT_WIDTH)), _resident((1, KV_WIDTH)), _resident((256, 256)),
                  pl.BlockSpec((TM, 128), rope_row), pl.BlockSpec((TM, 128), rope_row)],
        out_specs=[pl.BlockSpec((TM, ATT_WIDTH), row), pl.BlockSpec((TM, KV_WIDTH), row),
                   pl.BlockSpec((TM, KV_WIDTH), row), pl.BlockSpec((TM, SC_WIDTH), row),
                   pl.BlockSpec((TM, SC_WIDTH), row)],
        out_shape=[tok(ATT_WIDTH, BF16), tok(KV_WIDTH, F32), tok(KV_WIDTH, F32), tok(SC_WIDTH, BF16),
                   tok(SC_WIDTH, F32)],
        compiler_params=_params("arbitrary"),
        name="attn_in",
    )(x_p, x_s, mod, w_in.astype(BF16), jnp.tile(q_gain, N_HEADS)[None], jnp.tile(k_gain, N_KV)[None],
      _group_mean_matrix(HEAD_DIM), cos, sin)


def _attend_kernel(q_ref, *refs, n_kv_sets):
    kv_refs, o_ref = refs[:2 * n_kv_sets], refs[-1]
    ks = [r[...].astype(BF16) for r in kv_refs[:n_kv_sets]]
    vs = [r[...].astype(BF16) for r in kv_refs[n_kv_sets:]]
    q = q_ref[...]
    for hd in range(N_HEADS):
        g = hd // Q_GROUP
        qh = q[:, hd * HEAD_DIM:(hd + 1) * HEAD_DIM]
        s = [lax.dot_general(qh, kx[:, g * HEAD_DIM:(g + 1) * HEAD_DIM], NT_DIMS, preferred_element_type=F32)
             for kx in ks]
        mx = functools.reduce(jnp.maximum, [jnp.max(sx, axis=-1, keepdims=True) for sx in s])
        p = [jnp.exp(sx - mx) for sx in s]
        den = functools.reduce(jnp.add, [jnp.sum(px, axis=-1, keepdims=True) for px in p])
        acc = functools.reduce(jnp.add, [
            jnp.dot(px.astype(BF16), vx[:, g * HEAD_DIM:(g + 1) * HEAD_DIM], preferred_element_type=F32)
            for px, vx in zip(p, vs)])
        o_ref[:, hd * HEAD_DIM:(hd + 1) * HEAD_DIM] = (acc / den).astype(BF16)


def _attend_prompt(q, k, v):
    blk = lambda w: pl.BlockSpec((SEQ, w), lambda b: (b, 0))
    return pl.pallas_call(
        functools.partial(_attend_kernel, n_kv_sets=1),
        grid=(BATCH,),
        in_specs=[blk(ATT_WIDTH), blk(KV_WIDTH), blk(KV_WIDTH)],
        out_specs=blk(ATT_WIDTH),
        out_shape=jax.ShapeDtypeStruct((N_TOK, ATT_WIDTH), BF16),
        compiler_params=_params("arbitrary"),
        name="attend_prompt",
    )(q, k, v)


def _attend_latent(q, k, v, ctx_k, ctx_v, att):
    nq = DEC_SEQ // TQ
    q_off, kv_off = P_TOK // TQ, P_TOK // DEC_SEQ
    lat = pl.BlockSpec((DEC_SEQ, KV_WIDTH), lambda b, j: (kv_off + b, 0))
    ctx = pl.BlockSpec((None, PAST_LEN, KV_WIDTH), lambda b, j: (b, 0, 0))
    q_blk = pl.BlockSpec((TQ, ATT_WIDTH), lambda b, j: (q_off + b * nq + j, 0))
    return pl.pallas_call(
        functools.partial(_attend_kernel, n_kv_sets=2),
        grid=(DEC_BATCH, nq),
        in_specs=[q_blk, lat, ctx, lat, ctx, pl.BlockSpec(memory_space=pl.ANY)],
        out_specs=q_blk,
        out_shape=jax.ShapeDtypeStruct((N_TOK, ATT_WIDTH), BF16),
        input_output_aliases={5: 0},
        compiler_params=_params("arbitrary", "arbitrary"),
        name="attend_latent",
    )(q, k, ctx_k, v, ctx_v, att)


def _halo_specs(width):
    per = TM // HALO
    prev = pl.BlockSpec((HALO, width), lambda i: (jnp.maximum(i * per - 1, 0), 0))
    nxt = pl.BlockSpec((HALO, width), lambda i: (jnp.minimum((i + 1) * per, N_TOK // HALO - 1), 0))
    return prev, nxt


def _attn_out_kernel(xp_ref, xs_ref, mod_ref, att_ref, bg_ref, u_ref, up_ref, un_ref, cw_ref, w_ref, g_ref, b_ref,
                     o_ref):
    i = pl.program_id(0)
    m = mod_ref[0]
    pos, seq_len = _seq_pos(i, TM, 0)
    r = lax.broadcasted_iota(jnp.int32, (TM, 1), 0)
    u = u_ref[...]
    u_prev = jnp.where(r == 0, up_ref[HALO - 1:HALO, :], pltpu.roll(u, 1, 0))
    u_next = jnp.where(r == TM - 1, un_ref[0:1, :], pltpu.roll(u, TM - 1, 0))
    u_prev = jnp.where(pos == 0, 0.0, u_prev)
    u_next = jnp.where(pos == seq_len - 1, 0.0, u_next)
    cw = cw_ref[...]
    sc = bg_ref[...].astype(F32) * (cw[0:1] * u_prev + cw[1:2] * u + cw[2:3] * u_next)
    y = (jnp.dot(att_ref[...], w_ref[:ATT_WIDTH, :], preferred_element_type=F32)
         + jnp.dot(sc.astype(BF16), w_ref[ATT_WIDTH:, :], preferred_element_type=F32))
    x = jnp.where(i < P_BLOCKS, xp_ref[...], xs_ref[...])
    o_ref[...] = _residual_norm(x, y, m[2:3], g_ref[...], b_ref[...])


def _attn_out(x_p, x_s, mod, att, bg, u, conv_w, w_out, g, b):
    row = lambda i: (i, 0)
    up, un = _halo_specs(SC_WIDTH)
    return pl.pallas_call(
        _attn_out_kernel,
        grid=(N_BLOCKS,),
        in_specs=[*_split_specs(D_MODEL), _mod_spec(),
                  pl.BlockSpec((TM, ATT_WIDTH), row), pl.BlockSpec((TM, SC_WIDTH), row),
                  pl.BlockSpec((TM, SC_WIDTH), row), up, un,
                  _resident((CONV_WIDTH, SC_WIDTH)), _resident((D_MODEL, D_MODEL)),
                  _resident((1, D_MODEL)), _resident((1, D_MODEL))],
        out_specs=pl.BlockSpec((TM, D_MODEL), row),
        out_shape=jax.ShapeDtypeStruct((N_TOK, D_MODEL), F32),
        compiler_params=_params("arbitrary"),
        name="attn_out",
    )(x_p, x_s, mod, att, bg, u, u, u, conv_w, w_out.astype(BF16), g[None], b[None])


_FFN_FIRST_TILES = tuple((HALO + n * SEQ) // SUBLANES for n in range(TM // SEQ))
_FFN_LAST_TILES = tuple((HALO + n * SEQ - 1) // SUBLANES for n in range(1, TM // SEQ + 1))


def _ffn_kernel(x_ref, xp_ref, xn_ref, mod_ref, wu_ref, cw_ref, wd_ref, g_ref, b_ref, *refs, split_out):
    o_refs, (h_ref, acc_ref) = refs[:-2], refs[-2:]
    i = pl.program_id(0)
    m = mod_ref[0]
    rows = TM + 2 * HALO
    scale, shift = 1.0 + m[4:5], m[3:4]
    h_ref[:HALO, :] = (xp_ref[...] * scale + shift).astype(BF16)
    h_ref[HALO:HALO + TM, :] = (x_ref[...] * scale + shift).astype(BF16)
    h_ref[HALO + TM:, :] = (xn_ref[...] * scale + shift).astype(BF16)
    pos, seq_len = _seq_pos(i, rows, -HALO)
    keep_prev = (pos != 0).astype(F32)
    keep_next = (pos != seq_len - 1).astype(F32)
    acc_ref[...] = jnp.zeros_like(acc_ref)

    def conv(u, col):
        w = cw_ref[:, pl.ds(col, FF_CHUNK)]
        u_prev = _scale_tiles(pltpu.roll(u, 1, 0), keep_prev, _FFN_FIRST_TILES)
        u_next = _scale_tiles(pltpu.roll(u, rows - 1, 0), keep_next, _FFN_LAST_TILES)
        return (w[0:1] * u_prev + w[1:2] * u + w[2:3] * u_next)[HALO:HALO + TM]

    def step(c, carry):
        col_a = pl.multiple_of(c * FF_CHUNK, FF_CHUNK)
        col_g = pl.multiple_of(D_FF + c * FF_CHUNK, FF_CHUNK)
        h = h_ref[...]
        a = conv(jnp.dot(h, wu_ref[:, pl.ds(col_a, FF_CHUNK)], preferred_element_type=F32), col_a)
        gt = conv(jnp.dot(h, wu_ref[:, pl.ds(col_g, FF_CHUNK)], preferred_element_type=F32), col_g)
        act = (_silu(gt) * a).astype(BF16)
        acc_ref[...] += jnp.dot(act, wd_ref[pl.ds(col_a, FF_CHUNK), :], preferred_element_type=F32)
        return carry

    lax.fori_loop(0, D_FF // FF_CHUNK, step, 0)
    y = _residual_norm(x_ref[...], acc_ref[...], m[5:6], g_ref[...], b_ref[...])
    if split_out:
        @pl.when(i < P_BLOCKS)
        def _():
            o_refs[0][...] = y

        @pl.when(i >= P_BLOCKS)
        def _():
            o_refs[1][...] = y
    else:
        o_refs[0][...] = y


def _ffn(x, mod, w_up, conv_w, w_down, g, b, split_out):
    row = lambda i: (i, 0)
    xp, xn = _halo_specs(D_MODEL)
    if split_out:
        out_specs = list(_split_specs(D_MODEL))
        out_shape = [jax.ShapeDtypeStruct((P_TOK, D_MODEL), F32), jax.ShapeDtypeStruct((S_TOK, D_MODEL), F32)]
    else:
        out_specs = pl.BlockSpec((TM, D_MODEL), row)
        out_shape = jax.ShapeDtypeStruct((N_TOK, D_MODEL), F32)
    return pl.pallas_call(
        functools.partial(_ffn_kernel, split_out=split_out),
        grid=(N_BLOCKS,),
        in_specs=[pl.BlockSpec((TM, D_MODEL), row), xp, xn, _mod_spec(),
                  _resident((D_MODEL, 2 * D_FF)), _resident((CONV_WIDTH, 2 * D_FF)), _resident((D_FF, D_MODEL)),
                  _resident((1, D_MODEL)), _resident((1, D_MODEL))],
        out_specs=out_specs,
        out_shape=out_shape,
        scratch_shapes=[pltpu.VMEM((TM + 2 * HALO, D_MODEL), BF16), pltpu.VMEM((TM, D_MODEL), F32)],
        compiler_params=_params("arbitrary"),
        name="conv_ffn",
    )(x, x, x, mod, w_up.astype(BF16), conv_w, w_down.astype(BF16), g[None], b[None])


QIG_WIDTH = 3 * C_WIDTH
FF_WIDTH = 2 * C_WIDTH


def _hgrn_in_kernel(x_ref, mod_ref, w_ref, qig_ref, ff_ref):
    m = mod_ref[0]
    h = (x_ref[...] * (1.0 + m[1:2]) + m[0:1]).astype(BF16)
    for j in range(5):
        z = jnp.dot(h, w_ref[:, j * C_WIDTH:(j + 1) * C_WIDTH], preferred_element_type=F32)
        if j < 3:
            qig_ref[:, j * C_WIDTH:(j + 1) * C_WIDTH] = z.astype(BF16)
        else:
            ff_ref[:, (j - 3) * C_WIDTH:(j - 2) * C_WIDTH] = z


def _hgrn_in(x, mod, w_in):
    row = lambda i: (i, 0)
    return pl.pallas_call(
        _hgrn_in_kernel,
        grid=(N_BLOCKS,),
        in_specs=[pl.BlockSpec((TM, D_MODEL), row), _mod_spec(), _resident((D_MODEL, 5 * C_WIDTH))],
        out_specs=[pl.BlockSpec((TM, QIG_WIDTH), row), pl.BlockSpec((TM, FF_WIDTH), row)],
        out_shape=[jax.ShapeDtypeStruct((N_TOK, QIG_WIDTH), BF16), jax.ShapeDtypeStruct((N_TOK, FF_WIDTH), F32)],
        compiler_params=_params("arbitrary"),
        name="hgrn_in",
    )(x, mod, w_in.astype(BF16))


SCAN_LEVELS = tuple(1 << j for j in range(SCAN_CHUNK.bit_length() - 1))
FINE_LEVELS = tuple(h for h in SCAN_LEVELS if h < SUBLANES)
N_LEVELS = len(SCAN_LEVELS)
DIAG_CODE = 2 * N_LEVELS


def _is_query_side(t, half, forward):
    right = (t & half) != 0
    return right if forward else ~right


def _decay_matrix(forward):
    idx = np.arange(SCAN_CHUNK)
    t, s = idx[:, None], idx[None, :]
    cum = (s <= t) if forward else (s >= t)
    mats = [cum.astype(np.float32)]
    for half in FINE_LEVELS:
        ref = (idx // (2 * half)) * (2 * half) + (half - 1 if forward else half)
        sign = np.where(_is_query_side(idx, half, forward), 1.0, -1.0)[:, None]
        mats.append(sign * (cum.astype(np.float32) - cum[ref].astype(np.float32)))
    mat = np.concatenate(mats, axis=0)
    return jnp.asarray(np.concatenate([mat, mat], axis=1), dtype=BF16)


def _pair_codes():
    idx = np.arange(SCAN_CHUNK)
    t, s = idx[:, None], idx[None, :]
    lev = np.floor(np.log2(np.maximum(t ^ s, 1))).astype(np.int32)
    code = np.where(t > s, lev, np.where(t < s, N_LEVELS + lev, DIAG_CODE))
    return jnp.asarray(code, dtype=jnp.int32)


def _decay_stage(f_raw, lb, dmat2_ref):
    f = lb + (1.0 - lb) * _sigmoid(f_raw)
    g = jnp.log2(f)
    hi = g.astype(BF16)
    lo = (g - hi.astype(F32)).astype(BF16)
    return jnp.dot(dmat2_ref[...], jnp.concatenate([hi, lo], axis=0), preferred_element_type=F32), 1.0 - f


def _fine_levels(q, k, expo, code, forward, scores):
    C = SCAN_CHUNK
    t = lax.broadcasted_iota(jnp.int32, (C, 1), 0)
    base = 0 if forward else N_LEVELS
    for j, half in enumerate(FINE_LEVELS):
        w = jnp.exp2(expo[(j + 1) * C:(j + 2) * C])
        x = (jnp.where(_is_query_side(t, half, forward), q, k) * w).astype(BF16)
        prod = lax.dot_general(x, x, NT_DIMS, preferred_element_type=F32)
        scores = jnp.where(code == base + j, prod, scores)
    return scores


def _coarse_levels(q, k, cum, code, forward, rows):
    C = SCAN_CHUNK
    base = 0 if forward else N_LEVELS
    for j, half in enumerate(SCAN_LEVELS):
        if half in FINE_LEVELS:
            continue
        blk = 2 * half
        xq, xk, q_rows = [], [], []
        zero = jnp.zeros((half, C_DK), F32)
        for n in range(C // blk):
            lo, hi = slice(n * blk, n * blk + half), slice(n * blk + half, (n + 1) * blk)
            if forward:
                ref = cum[n * blk + half - 1:n * blk + half]
                xk += [k[lo] * jnp.exp2(ref - cum[lo]), zero]
                xq.append(q[hi] * jnp.exp2(cum[hi] - ref))
                q_rows.append(n * blk + half)
            else:
                ref = cum[n * blk + half:n * blk + half + 1]
                xq.append(q[lo] * jnp.exp2(cum[lo] - ref))
                xk += [zero, k[hi] * jnp.exp2(ref - cum[hi])]
                q_rows.append(n * blk)
        prod = lax.dot_general(jnp.concatenate(xq, axis=0).astype(BF16), jnp.concatenate(xk, axis=0).astype(BF16),
                               NT_DIMS, preferred_element_type=F32)
        for n, r0 in enumerate(q_rows):
            for i in range(half // SUBLANES):
                rb, src = r0 // SUBLANES + i, n * half + SUBLANES * i
                own = code[SUBLANES * rb:SUBLANES * (rb + 1)] == base + j
                rows[rb] = jnp.where(own, prod[src:src + SUBLANES], rows[rb])
    return rows


def _hgrn_scan_kernel(*refs, sub_len, has_init, aliased):
    q_ref, v_ref, ff_ref, fb_ref, lbl_ref, dmf_ref, dmb_ref, code_ref = refs[:8]
    refs = refs[8:]
    if has_init:
        s0_ref, refs = refs[0], refs[1:]
    if aliased:
        refs = refs[1:]
    o_ref, refs = refs[0], refs[1:]
    if not has_init:
        sout_ref, refs = refs[0], refs[1:]
    sc_ref, qdf_ref, qdb_ref, decf_ref, decb_ref, incf_ref, incb_ref, stf_ref, stb_ref = refs
    C = SCAN_CHUNK
    n_chunks = SCAN_BLOCK // C
    sub_chunks = sub_len // C

    lg = lbl_ref[...]
    ex = jnp.exp(lg - jnp.max(lg, axis=0, keepdims=True))
    prob = ex / jnp.sum(ex, axis=0, keepdims=True)
    lb = jnp.sum(prob[1:DEPTH], axis=0)
    lb_f, lb_b = lb[0:1], lb[1:2]
    code = code_ref[...]

    def rows_of(c):
        return pl.ds(pl.multiple_of(c * C, C), C)

    def local_step(i, carry):
        (expo_f, k_f, expo_b, k_b), (kd_f, kd_b) = carry
        c3 = jnp.clip(i - 2, 0, n_chunks - 1)
        v3 = v_ref[rows_of(c3), :]
        incf_ref[c3] = lax.dot_general(v3, kd_f, TN_DIMS, preferred_element_type=F32)
        incb_ref[c3] = lax.dot_general(v3, kd_b, TN_DIMS, preferred_element_type=F32)
        c2 = jnp.clip(i - 1, 0, n_chunks - 1)
        r2 = rows_of(c2)
        q = q_ref[r2, :].astype(F32)
        cum_f, cum_b = expo_f[:C], expo_b[:C]
        scores = _fine_levels(q, k_f, expo_f, code, True, jnp.zeros((C, C), F32))
        scores = _fine_levels(q, k_b, expo_b, code, False, scores)
        rows = [scores[SUBLANES * n:SUBLANES * (n + 1)] for n in range(C // SUBLANES)]
        rows = _coarse_levels(q, k_f, cum_f, code, True, rows)
        rows = _coarse_levels(q, k_b, cum_b, code, False, rows)
        same_row = jnp.sum(q * (k_f + k_b), axis=-1, keepdims=True)
        sc_ref[c2] = jnp.where(code == DIAG_CODE, same_row, jnp.concatenate(rows, axis=0)).astype(BF16)
        last_f, last_b = cum_f[C - 1:C, :], cum_b[0:1, :]
        qdf_ref[r2, :] = (q * jnp.exp2(cum_f)).astype(BF16)
        qdb_ref[r2, :] = (q * jnp.exp2(cum_b)).astype(BF16)
        decf_ref[pl.ds(c2, 1), :] = jnp.exp2(last_f)
        decb_ref[pl.ds(c2, 1), :] = jnp.exp2(last_b)
        nxt_kd = ((k_f * jnp.exp2(last_f - cum_f)).astype(BF16), (k_b * jnp.exp2(last_b - cum_b)).astype(BF16))
        return decays(jnp.minimum(i, n_chunks - 1)), nxt_kd

    def decays(c):
        r1 = rows_of(c)
        return _decay_stage(ff_ref[r1, :], lb_f, dmf_ref) + _decay_stage(fb_ref[r1, :], lb_b, dmb_ref)

    placeholder = q_ref[rows_of(0), :]
    lax.fori_loop(1, n_chunks + 2, local_step, (decays(0), (placeholder, placeholder)))

    for sub in range(SCAN_BLOCK // sub_len):
        first = sub * sub_chunks
        if has_init:
            init = (s0_ref[0].T, s0_ref[1].T)
        else:
            init = (jnp.zeros((C_DV, C_DK), F32),) * 2

        def state_step(c, carry):
            st_f, st_b = carry
            cf, cb = first + c, first + sub_chunks - 1 - c
            stf_ref[cf] = st_f.astype(BF16)
            stb_ref[cb] = st_b.astype(BF16)
            return (st_f * decf_ref[pl.ds(cf, 1), :] + incf_ref[cf], st_b * decb_ref[pl.ds(cb, 1), :] + incb_ref[cb])

        st_f, st_b = lax.fori_loop(0, sub_chunks, state_step, init, unroll=2)
        if not has_init:
            sout_ref[sub, 0] = st_f.T
            sout_ref[sub, 1] = st_b.T

    def out_step(c, carry):
        r = rows_of(c)
        o_ref[r, :] = (jnp.dot(sc_ref[c], v_ref[r, :], preferred_element_type=F32)
                       + lax.dot_general(qdf_ref[r, :], stf_ref[c], NT_DIMS, preferred_element_type=F32)
                       + lax.dot_general(qdb_ref[r, :], stb_ref[c], NT_DIMS, preferred_element_type=F32))
        return carry

    lax.fori_loop(0, n_chunks, out_step, 0, unroll=2)


def _hgrn_scan(qig, ff, lb_logits, sub_len, row_off, n_rows, s0, o_prev):
    n_blocks = n_rows // SCAN_BLOCK
    blk_off = row_off // SCAN_BLOCK
    n_sub = SCAN_BLOCK // sub_len
    heads = C_WIDTH // C_DK
    n_chunks = SCAN_BLOCK // SCAN_CHUNK
    col = lambda part: pl.BlockSpec((SCAN_BLOCK, C_DK), lambda n, h: (blk_off + n, part * heads + h))
    dmat_rows = (1 + len(FINE_LEVELS)) * SCAN_CHUNK
    in_specs = [col(0), col(1), col(0), col(1),
                pl.BlockSpec((DEPTH, 2, C_DK), lambda n, h: (0, 0, h)),
                pl.BlockSpec((dmat_rows, 2 * SCAN_CHUNK), lambda n, h: (0, 0)),
                pl.BlockSpec((dmat_rows, 2 * SCAN_CHUNK), lambda n, h: (0, 0)),
                pl.BlockSpec((SCAN_CHUNK, SCAN_CHUNK), lambda n, h: (0, 0))]
    args = [qig, qig, ff, ff, lb_logits, _decay_matrix(True), _decay_matrix(False), _pair_codes()]
    state_spec = lambda n_seq: pl.BlockSpec((n_seq, 2, None, C_DK, C_DV), lambda n, h: (n, 0, h, 0, 0))
    if s0 is not None:
        assert n_sub == 1
        in_specs.append(pl.BlockSpec((None, 2, None, C_DK, C_DV), lambda n, h: (n, 0, h, 0, 0)))
        args.append(s0)
    aliases = {}
    if o_prev is not None:
        aliases = {len(args): 0}
        in_specs.append(pl.BlockSpec(memory_space=pl.ANY))
        args.append(o_prev)
    out_specs = [pl.BlockSpec((SCAN_BLOCK, C_DV), lambda n, h: (blk_off + n, h))]
    out_shape = [jax.ShapeDtypeStruct((N_TOK, C_WIDTH), F32)]
    if s0 is None:
        out_specs.append(state_spec(n_sub))
        out_shape.append(jax.ShapeDtypeStruct((n_blocks * n_sub, 2, heads, C_DK, C_DV), F32))
    chunk_mats = lambda dt: pltpu.VMEM((n_chunks, SCAN_CHUNK, SCAN_CHUNK), dt)
    return pl.pallas_call(
        functools.partial(_hgrn_scan_kernel, sub_len=sub_len, has_init=s0 is not None, aliased=o_prev is not None),
        grid=(n_blocks, heads),
        in_specs=in_specs, out_specs=out_specs, out_shape=out_shape,
        input_output_aliases=aliases,
        scratch_shapes=[chunk_mats(BF16), pltpu.VMEM((SCAN_BLOCK, C_DK), BF16), pltpu.VMEM((SCAN_BLOCK, C_DK), BF16),
                        pltpu.VMEM((n_chunks, C_DK), F32), pltpu.VMEM((n_chunks, C_DK), F32),
                        chunk_mats(F32), chunk_mats(F32), chunk_mats(BF16), chunk_mats(BF16)],
        compiler_params=_params("arbitrary", "arbitrary"),
        name="hgrn_scan_init" if s0 is not None else "hgrn_scan_zero",
    )(*args)


def _hgrn_out_kernel(x_ref, mod_ref, o_ref, gate_ref, ng_ref, gmat_ref, w_ref, g_ref, b_ref, out_ref):
    m = mod_ref[0]
    o = o_ref[...]
    o = o * lax.rsqrt(_group_mean_square(o, gmat_ref) + EPS) * ng_ref[...] * _silu(gate_ref[...].astype(F32))
    y = jnp.dot(o.astype(BF16), w_ref[...], preferred_element_type=F32)
    out_ref[...] = _residual_norm(x_ref[...], y, m[2:3], g_ref[...], b_ref[...])


def _hgrn_out(x, mod, o, qig, norm_g, w_out, g, b):
    row = lambda i: (i, 0)
    return pl.pallas_call(
        _hgrn_out_kernel,
        grid=(N_BLOCKS,),
        in_specs=[pl.BlockSpec((TM, D_MODEL), row), _mod_spec(),
                  pl.BlockSpec((TM, C_WIDTH), row), pl.BlockSpec((TM, C_WIDTH), lambda i: (i, 2)),
                  _resident((1, C_WIDTH)), _resident((256, 256)), _resident((C_WIDTH, D_MODEL)),
                  _resident((1, D_MODEL)), _resident((1, D_MODEL))],
        out_specs=pl.BlockSpec((TM, D_MODEL), row),
        out_shape=jax.ShapeDtypeStruct((N_TOK, D_MODEL), F32),
        compiler_params=_params("arbitrary"),
        name="hgrn_out",
    )(x, mod, o, qig, jnp.tile(norm_g, C_HEADS)[None], _group_mean_matrix(C_DV), w_out.astype(BF16), g[None], b[None])


def kernel(x_prompt, x_sample, cache_k, cache_v, state_hgrn, c, c_ctx, w_mod, b_mod, ln1_g, ln1_b, ln2_g, ln2_b,
           attn_w_in, attn_q_gain, attn_k_gain, sconv_w, attn_w_out, hgrn_w_in, hgrn_lb_logits, hgrn_norm_g,
           hgrn_w_out, ffn_w_up, ffn_conv_w, ffn_w_down):
    x_p = x_prompt.reshape(P_TOK, D_MODEL)
    x_s = x_sample.reshape(S_TOK, D_MODEL)
    cond = jnp.concatenate([c_ctx[None], c, jnp.zeros((MOD_ROWS - 1 - DEC_BATCH, D_MODEL), F32)], axis=0)
    mod = _modulation(cond, w_mod, b_mod).reshape(DEPTH, MOD_ROWS, 6, D_MODEL)

    q, k, v, bg, u = _attn_in(x_p, x_s, mod[0], attn_w_in[0], attn_q_gain[0], attn_k_gain[0])
    att = _attend_prompt(q, k, v)
    att = _attend_latent(q, k, v, cache_k[:, 0].reshape(DEC_BATCH, PAST_LEN, KV_WIDTH),
                         cache_v[:, 0].reshape(DEC_BATCH, PAST_LEN, KV_WIDTH), att)
    x = _attn_out(x_p, x_s, mod[0], att, bg, u, sconv_w[0], attn_w_out[0], ln1_g[0], ln1_b[0])
    x = _ffn(x, mod[0], ffn_w_up[0], ffn_conv_w[0], ffn_w_down[0], ln2_g[0], ln2_b[0], split_out=False)
    new_k = k[:P_TOK].reshape(BATCH, 1, SEQ, N_KV, HEAD_DIM)
    new_v = v[:P_TOK].reshape(BATCH, 1, SEQ, N_KV, HEAD_DIM)

    qig, ff = _hgrn_in(x, mod[1], hgrn_w_in[0])
    o, s_new = _hgrn_scan(qig, ff, hgrn_lb_logits, SEQ, 0, P_TOK, None, None)
    o, = _hgrn_scan(qig, ff, hgrn_lb_logits, DEC_SEQ, P_TOK, S_TOK, state_hgrn[:, 0], o)
    x = _hgrn_out(x, mod[1], o, qig, hgrn_norm_g[0], hgrn_w_out[0], ln1_g[1], ln1_b[1])
    y_p, y_s = _ffn(x, mod[1], ffn_w_up[1], ffn_conv_w[1], ffn_w_down[1], ln2_g[1], ln2_b[1], split_out=True)

    return (y_p.reshape(BATCH, SEQ, D_MODEL), y_s.reshape(DEC_BATCH, DEC_SEQ, D_MODEL), new_k, new_v, s_new[:, None])
```

```python
import functools

import jax
import jax.numpy as jnp
import numpy as np
from jax import lax
from jax.experimental import pallas as pl
from jax.experimental.pallas import tpu as pltpu

F32 = jnp.float32
BF16 = jnp.bfloat16

D_MODEL = 1024
BATCH = 32
SEQ = 256
DEPTH = 2
DEC_BATCH = 4
DEC_SEQ = 2048
PAST_LEN = 512
GRID_W = 64
N_HEADS = 8
N_KV = 2
HEAD_DIM = 64
Q_GROUP = N_HEADS // N_KV
ATT_WIDTH = N_HEADS * HEAD_DIM
KV_WIDTH = N_KV * HEAD_DIM
ROPE_AXIS_DIM = HEAD_DIM // 2
ROPE_THETA = 10000.0
SC_WIDTH = D_MODEL - ATT_WIDTH
CONV_WIDTH = 3
C_HEADS = 8
C_DK = D_MODEL // C_HEADS
C_DV = D_MODEL // C_HEADS
C_WIDTH = C_HEADS * C_DK
D_FF = 2816
EVEN_IN_WIDTH = ATT_WIDTH + 2 * KV_WIDTH + 3 * SC_WIDTH
ALPHA = (2 * DEPTH) ** 0.25
EPS = 1e-6

P_TOK = BATCH * SEQ
S_TOK = DEC_BATCH * DEC_SEQ
N_TOK = P_TOK + S_TOK
MOD_ROWS = 8
SUBLANES = 8
TM = 512
HALO = SUBLANES
FF_CHUNK = 256
SCAN_CHUNK = 128
SCAN_BLOCK = DEC_SEQ
TQ = 256
VMEM_LIMIT = 56 * 1024 * 1024

assert P_TOK % TM == 0 and S_TOK % TM == 0 and DEC_SEQ % TM == 0 and TM % SEQ == 0 and DEC_SEQ % SEQ == 0
assert D_FF % FF_CHUNK == 0 and SEQ % SCAN_CHUNK == 0 and DEC_SEQ % SCAN_CHUNK == 0
assert SCAN_BLOCK % SEQ == 0 and P_TOK % SCAN_BLOCK == 0
P_BLOCKS = P_TOK // TM
N_BLOCKS = N_TOK // TM
S_BLOCKS_PER_SEQ = DEC_SEQ // TM

NT_DIMS = (((1,), (1,)), ((), ()))
TN_DIMS = (((0,), (0,)), ((), ()))


def _params(*sem):
    return pltpu.CompilerParams(dimension_semantics=sem, vmem_limit_bytes=VMEM_LIMIT)


def _mod_row(i):
    return jnp.where(i < P_BLOCKS, 0, 1 + (i - P_BLOCKS) // S_BLOCKS_PER_SEQ)


def _mod_spec():
    return pl.BlockSpec((1, 6, D_MODEL), lambda i: (_mod_row(i), 0, 0))


def _resident(shape):
    nd = len(shape)
    return pl.BlockSpec(shape, lambda *_: (0,) * nd, pipeline_mode=pl.Buffered(1))


def _split_specs(width):
    return (pl.BlockSpec((TM, width), lambda i: (jnp.minimum(i, P_BLOCKS - 1), 0)),
            pl.BlockSpec((TM, width), lambda i: (jnp.maximum(i - P_BLOCKS, 0), 0)))


def _sigmoid(x):
    return 0.5 * jnp.tanh(0.5 * x) + 0.5


def _silu(x):
    hx = 0.5 * x
    return hx * jnp.tanh(hx) + hx


def _layer_norm(x, g, b):
    mu = jnp.mean(x, axis=-1, keepdims=True)
    xc = x - mu
    var = jnp.mean(xc * xc, axis=-1, keepdims=True)
    return xc * lax.rsqrt(var + EPS) * g + b


def _residual_norm(x, y, gate, g, b):
    return _layer_norm(ALPHA * x + gate * y, g, b)


def _group_mean_matrix(group):
    idx = np.arange(256)
    return jnp.asarray((idx[:, None] // group == idx[None, :] // group) / group, dtype=BF16)


def _group_mean_square(x, gmat_ref):
    sq = (x * x).astype(BF16)
    n = x.shape[1] // 256
    parts = [jnp.dot(sq[:, 256 * j:256 * (j + 1)], gmat_ref[...], preferred_element_type=F32) for j in range(n)]
    return parts[0] if n == 1 else jnp.concatenate(parts, axis=1)


def _seq_pos(i, rows, first_row):
    seq_len = jnp.where(i < P_BLOCKS, SEQ, DEC_SEQ)
    r = lax.broadcasted_iota(jnp.int32, (rows, 1), 0) + (i * TM + first_row)
    return r & (seq_len - 1), seq_len


def _scale_tiles(x, factor, tiles):
    parts, pos = [], 0
    for t in sorted(tiles):
        lo = t * SUBLANES
        if lo > pos:
            parts.append(x[pos:lo])
        parts.append(x[lo:lo + SUBLANES] * factor[lo:lo + SUBLANES])
        pos = lo + SUBLANES
    if pos < x.shape[0]:
        parts.append(x[pos:])
    return jnp.concatenate(parts, axis=0)


MOD_TN = 1536


def _mod_kernel(cond_ref, w_ref, b_ref, o_ref):
    s = _silu(cond_ref[...])
    o_ref[...] = jnp.dot(s, w_ref[...], preferred_element_type=F32) + b_ref[...]


def _modulation(cond, w_mod, b_mod):
    n_out = 6 * D_MODEL
    return pl.pallas_call(
        _mod_kernel,
        grid=(DEPTH, n_out // MOD_TN),
        in_specs=[pl.BlockSpec((MOD_ROWS, D_MODEL), lambda l, j: (0, 0)),
                  pl.BlockSpec((None, D_MODEL, MOD_TN), lambda l, j: (l, 0, j)),
                  pl.BlockSpec((None, 1, MOD_TN), lambda l, j: (l, 0, j))],
        out_specs=pl.BlockSpec((None, MOD_ROWS, MOD_TN), lambda l, j: (l, 0, j)),
        out_shape=jax.ShapeDtypeStruct((DEPTH, MOD_ROWS, n_out), F32),
        compiler_params=_params("arbitrary", "arbitrary"),
        name="modulation",
    )(cond, w_mod, b_mod.reshape(DEPTH, 1, n_out))


def _rope_tables():
    t = np.arange(DEC_SEQ)
    half = ROPE_AXIS_DIM // 2
    inv = (ROPE_THETA ** (-np.arange(0, ROPE_AXIS_DIM, 2, dtype=np.float32) / ROPE_AXIS_DIM)).astype(np.float32)
    row = (t // GRID_W).astype(np.float32)[:, None] * inv
    col = (t % GRID_W).astype(np.float32)[:, None] * inv
    ang = np.concatenate([row, row, col, col], axis=1).astype(np.float32)
    sign = np.concatenate([-np.ones(half), np.ones(half)] * 2).astype(np.float32)
    cos = np.tile(np.cos(ang), (1, 128 // HEAD_DIM))
    sin = np.tile(np.sin(ang) * sign, (1, 128 // HEAD_DIM))
    return jnp.asarray(cos, F32), jnp.asarray(sin, F32)


def _rope(x, cos, sin):
    n = x.shape[1] // 128
    if n > 1:
        cos = jnp.concatenate([cos] * n, axis=1)
        sin = jnp.concatenate([sin] * n, axis=1)
    lane = lax.broadcasted_iota(jnp.int32, x.shape, 1)
    half = ROPE_AXIS_DIM // 2
    partner = jnp.where((lane & (ROPE_AXIS_DIM - 1)) < half,
                        pltpu.roll(x, x.shape[1] - half, 1), pltpu.roll(x, half, 1))
    return x * cos + partner * sin


def _attn_in_kernel(xp_ref, xs_ref, mod_ref, w_ref, qg_ref, kg_ref, gmat_ref, cos_ref, sin_ref,
                    q_ref, k_ref, v_ref, bg_ref, u_ref):
    i = pl.program_id(0)
    m = mod_ref[0]
    x = jnp.where(i < P_BLOCKS, xp_ref[...], xs_ref[...])
    h = (x * (1.0 + m[1:2]) + m[0:1]).astype(BF16)
    z = jnp.dot(h, w_ref[...], preferred_element_type=F32)
    q = z[:, :ATT_WIDTH]
    k = z[:, ATT_WIDTH:ATT_WIDTH + KV_WIDTH]
    o = ATT_WIDTH + 2 * KV_WIDTH
    q = q * lax.rsqrt(_group_mean_square(q, gmat_ref) + EPS) * qg_ref[...]
    kk = (k * k).astype(BF16)
    k_ms = jnp.dot(kk, gmat_ref[:KV_WIDTH, :KV_WIDTH], preferred_element_type=F32)
    k = k * lax.rsqrt(k_ms + EPS) * kg_ref[...]
    v_ref[...] = z[:, ATT_WIDTH + KV_WIDTH:o]
    bg_ref[...] = z[:, o:o + SC_WIDTH].astype(BF16)
    u_ref[...] = z[:, o + SC_WIDTH:o + 2 * SC_WIDTH] * z[:, o + 2 * SC_WIDTH:]

    @pl.when(i < P_BLOCKS)
    def _():
        q_ref[...] = (q * HEAD_DIM ** -0.5).astype(BF16)
        k_ref[...] = k

    @pl.when(i >= P_BLOCKS)
    def _():
        cos, sin = cos_ref[...], sin_ref[...]
        q_ref[...] = (_rope(q, cos, sin) * HEAD_DIM ** -0.5).astype(BF16)
        k_ref[...] = _rope(k, cos, sin)


def _attn_in(x_p, x_s, mod, w_in, q_gain, k_gain):
    cos, sin = _rope_tables()
    row = lambda i: (i, 0)
    rope_row = lambda i: (jnp.maximum(i - P_BLOCKS, 0) % S_BLOCKS_PER_SEQ, 0)
    tok = lambda w, dt: jax.ShapeDtypeStruct((N_TOK, w), dt)
    return pl.pallas_call(
        _attn_in_kernel,
        grid=(N_BLOCKS,),
        in_specs=[*_split_specs(D_MODEL), _mod_spec(),
                  _resident((D_MODEL, EVEN_IN_WIDTH)),
                  _resident((1, ATT_WIDTH)), _resident((1, KV_WIDTH)), _resident((256, 256)),
                  pl.BlockSpec((TM, 128), rope_row), pl.BlockSpec((TM, 128), rope_row)],
        out_specs=[pl.BlockSpec((TM, ATT_WIDTH), row), pl.BlockSpec((TM, KV_WIDTH), row),
                   pl.BlockSpec((TM, KV_WIDTH), row), pl.BlockSpec((TM, SC_WIDTH), row),
                   pl.BlockSpec((TM, SC_WIDTH), row)],
        out_shape=[tok(ATT_WIDTH, BF16), tok(KV_WIDTH, F32), tok(KV_WIDTH, F32), tok(SC_WIDTH, BF16),
                   tok(SC_WIDTH, F32)],
        compiler_params=_params("arbitrary"),
        name="attn_in",
    )(x_p, x_s, mod, w_in.astype(BF16), jnp.tile(q_gain, N_HEADS)[None], jnp.tile(k_gain, N_KV)[None],
      _group_mean_matrix(HEAD_DIM), cos, sin)


def _attend_kernel(q_ref, *refs, n_kv_sets):
    kv_refs, o_ref = refs[:2 * n_kv_sets], refs[-1]
    ks = [r[...].astype(BF16) for r in kv_refs[:n_kv_sets]]
    vs = [r[...].astype(BF16) for r in kv_refs[n_kv_sets:]]
    q = q_ref[...]
    for hd in range(N_HEADS):
        g = hd // Q_GROUP
        qh = q[:, hd * HEAD_DIM:(hd + 1) * HEAD_DIM]
        s = [lax.dot_general(qh, kx[:, g * HEAD_DIM:(g + 1) * HEAD_DIM], NT_DIMS, preferred_element_type=F32)
             for kx in ks]
        mx = functools.reduce(jnp.maximum, [jnp.max(sx, axis=-1, keepdims=True) for sx in s])
        p = [jnp.exp(sx - mx) for sx in s]
        den = functools.reduce(jnp.add, [jnp.sum(px, axis=-1, keepdims=True) for px in p])
        acc = functools.reduce(jnp.add, [
            jnp.dot(px.astype(BF16), vx[:, g * HEAD_DIM:(g + 1) * HEAD_DIM], preferred_element_type=F32)
            for px, vx in zip(p, vs)])
        o_ref[:, hd * HEAD_DIM:(hd + 1) * HEAD_DIM] = (acc / den).astype(BF16)


def _attend_prompt(q, k, v):
    blk = lambda w: pl.BlockSpec((SEQ, w), lambda b: (b, 0))
    return pl.pallas_call(
        functools.partial(_attend_kernel, n_kv_sets=1),
        grid=(BATCH,),
        in_specs=[blk(ATT_WIDTH), blk(KV_WIDTH), blk(KV_WIDTH)],
        out_specs=blk(ATT_WIDTH),
        out_shape=jax.ShapeDtypeStruct((N_TOK, ATT_WIDTH), BF16),
        compiler_params=_params("arbitrary"),
        name="attend_prompt",
    )(q, k, v)


def _attend_latent(q, k, v, ctx_k, ctx_v, att):
    nq = DEC_SEQ // TQ
    q_off, kv_off = P_TOK // TQ, P_TOK // DEC_SEQ
    lat = pl.BlockSpec((DEC_SEQ, KV_WIDTH), lambda b, j: (kv_off + b, 0))
    ctx = pl.BlockSpec((None, PAST_LEN, KV_WIDTH), lambda b, j: (b, 0, 0))
    q_blk = pl.BlockSpec((TQ, ATT_WIDTH), lambda b, j: (q_off + b * nq + j, 0))
    return pl.pallas_call(
        functools.partial(_attend_kernel, n_kv_sets=2),
        grid=(DEC_BATCH, nq),
        in_specs=[q_blk, lat, ctx, lat, ctx, pl.BlockSpec(memory_space=pl.ANY)],
        out_specs=q_blk,
        out_shape=jax.ShapeDtypeStruct((N_TOK, ATT_WIDTH), BF16),
        input_output_aliases={5: 0},
        compiler_params=_params("arbitrary", "arbitrary"),
        name="attend_latent",
    )(q, k, ctx_k, v, ctx_v, att)


def _halo_specs(width):
    per = TM // HALO
    prev = pl.BlockSpec((HALO, width), lambda i: (jnp.maximum(i * per - 1, 0), 0))
    nxt = pl.BlockSpec((HALO, width), lambda i: (jnp.minimum((i + 1) * per, N_TOK // HALO - 1), 0))
    return prev, nxt


def _attn_out_kernel(xp_ref, xs_ref, mod_ref, att_ref, bg_ref, u_ref, up_ref, un_ref, cw_ref, w_ref, g_ref, b_ref,
                     o_ref):
    i = pl.program_id(0)
    m = mod_ref[0]
    pos, seq_len = _seq_pos(i, TM, 0)
    r = lax.broadcasted_iota(jnp.int32, (TM, 1), 0)
    u = u_ref[...]
    u_prev = jnp.where(r == 0, up_ref[HALO - 1:HALO, :], pltpu.roll(u, 1, 0))
    u_next = jnp.where(r == TM - 1, un_ref[0:1, :], pltpu.roll(u, TM - 1, 0))
    u_prev = jnp.where(pos == 0, 0.0, u_prev)
    u_next = jnp.where(pos == seq_len - 1, 0.0, u_next)
    cw = cw_ref[...]
    sc = bg_ref[...].astype(F32) * (cw[0:1] * u_prev + cw[1:2] * u + cw[2:3] * u_next)
    y = (jnp.dot(att_ref[...], w_ref[:ATT_WIDTH, :], preferred_element_type=F32)
         + jnp.dot(sc.astype(BF16), w_ref[ATT_WIDTH:, :], preferred_element_type=F32))
    x = jnp.where(i < P_BLOCKS, xp_ref[...], xs_ref[...])
    o_ref[...] = _residual_norm(x, y, m[2:3], g_ref[...], b_ref[...])


def _attn_out(x_p, x_s, mod, att, bg, u, conv_w, w_out, g, b):
    row = lambda i: (i, 0)
    up, un = _halo_specs(SC_WIDTH)
    return pl.pallas_call(
        _attn_out_kernel,
        grid=(N_BLOCKS,),
        in_specs=[*_split_specs(D_MODEL), _mod_spec(),
                  pl.BlockSpec((TM, ATT_WIDTH), row), pl.BlockSpec((TM, SC_WIDTH), row),
                  pl.BlockSpec((TM, SC_WIDTH), row), up, un,
                  _resident((CONV_WIDTH, SC_WIDTH)), _resident((D_MODEL, D_MODEL)),
                  _resident((1, D_MODEL)), _resident((1, D_MODEL))],
        out_specs=pl.BlockSpec((TM, D_MODEL), row),
        out_shape=jax.ShapeDtypeStruct((N_TOK, D_MODEL), F32),
        compiler_params=_params("arbitrary"),
        name="attn_out",
    )(x_p, x_s, mod, att, bg, u, u, u, conv_w, w_out.astype(BF16), g[None], b[None])


_FFN_FIRST_TILES = tuple((HALO + n * SEQ) // SUBLANES for n in range(TM // SEQ))
_FFN_LAST_TILES = tuple((HALO + n * SEQ - 1) // SUBLANES for n in range(1, TM // SEQ + 1))


def _ffn_kernel(x_ref, xp_ref, xn_ref, mod_ref, wu_ref, cw_ref, wd_ref, g_ref, b_ref, *refs, split_out):
    o_refs, (h_ref, u_ref, act_ref) = refs[:-3], refs[-3:]
    i = pl.program_id(0)
    m = mod_ref[0]
    rows = TM + 2 * HALO
    scale, shift = 1.0 + m[4:5], m[3:4]
    h_ref[:HALO, :] = (xp_ref[...] * scale + shift).astype(BF16)
    h_ref[HALO:HALO + TM, :] = (x_ref[...] * scale + shift).astype(BF16)
    h_ref[HALO + TM:, :] = (xn_ref[...] * scale + shift).astype(BF16)
    pos, seq_len = _seq_pos(i, rows, -HALO)
    keep_prev = (pos != 0).astype(F32)
    keep_next = (pos != seq_len - 1).astype(F32)
    n_chunks = D_FF // FF_CHUNK

    def cols(c):
        return pl.multiple_of(c * FF_CHUNK, FF_CHUNK), pl.multiple_of(D_FF + c * FF_CHUNK, FF_CHUNK)

    def up(c, slot):
        h = h_ref[...]
        for part, col in enumerate(cols(c)):
            u_ref[slot, part] = jnp.dot(h, wu_ref[:, pl.ds(col, FF_CHUNK)], preferred_element_type=F32)

    def conv(slot, part, col):
        w = cw_ref[:, pl.ds(col, FF_CHUNK)]
        u = u_ref[slot, part]
        u_prev = _scale_tiles(pltpu.roll(u, 1, 0), keep_prev, _FFN_FIRST_TILES)
        u_next = _scale_tiles(pltpu.roll(u, rows - 1, 0), keep_next, _FFN_LAST_TILES)
        return (w[0:1] * u_prev + w[1:2] * u + w[2:3] * u_next)[HALO:HALO + TM]

    def gate(c, slot):
        col_a, col_g = cols(c)
        act_ref[:, pl.ds(col_a, FF_CHUNK)] = (_silu(conv(slot, 1, col_g)) * conv(slot, 0, col_a)).astype(BF16)

    def step(j, carry):
        c = 1 + 2 * j
        for d, slot in ((0, 1), (1, 0)):
            gate(c + d - 1, 1 - slot)
            up(c + d, slot)
        return carry

    assert n_chunks % 2 == 1
    up(0, 0)
    lax.fori_loop(0, (n_chunks - 1) // 2, step, 0)
    gate(n_chunks - 1, 0)
    y = jnp.dot(act_ref[...], wd_ref[...], preferred_element_type=F32)
    y = _residual_norm(x_ref[...], y, m[5:6], g_ref[...], b_ref[...])
    if split_out:
        @pl.when(i < P_BLOCKS)
        def _():
            o_refs[0][...] = y

        @pl.when(i >= P_BLOCKS)
        def _():
            o_refs[1][...] = y
    else:
        o_refs[0][...] = y


def _ffn(x, mod, w_up, conv_w, w_down, g, b, split_out):
    row = lambda i: (i, 0)
    xp, xn = _halo_specs(D_MODEL)
    if split_out:
        out_specs = list(_split_specs(D_MODEL))
        out_shape = [jax.ShapeDtypeStruct((P_TOK, D_MODEL), F32), jax.ShapeDtypeStruct((S_TOK, D_MODEL), F32)]
    else:
        out_specs = pl.BlockSpec((TM, D_MODEL), row)
        out_shape = jax.ShapeDtypeStruct((N_TOK, D_MODEL), F32)
    return pl.pallas_call(
        functools.partial(_ffn_kernel, split_out=split_out),
        grid=(N_BLOCKS,),
        in_specs=[pl.BlockSpec((TM, D_MODEL), row), xp, xn, _mod_spec(),
                  _resident((D_MODEL, 2 * D_FF)), _resident((CONV_WIDTH, 2 * D_FF)), _resident((D_FF, D_MODEL)),
                  _resident((1, D_MODEL)), _resident((1, D_MODEL))],
        out_specs=out_specs,
        out_shape=out_shape,
        scratch_shapes=[pltpu.VMEM((TM + 2 * HALO, D_MODEL), BF16),
                        pltpu.VMEM((2, 2, TM + 2 * HALO, FF_CHUNK), F32), pltpu.VMEM((TM, D_FF), BF16)],
        compiler_params=_params("arbitrary"),
        name="conv_ffn",
    )(x, x, x, mod, w_up.astype(BF16), conv_w, w_down.astype(BF16), g[None], b[None])


QIG_WIDTH = 3 * C_WIDTH
FF_WIDTH = 2 * C_WIDTH


def _hgrn_in_kernel(x_ref, mod_ref, w_ref, qig_ref, ff_ref):
    m = mod_ref[0]
    h = (x_ref[...] * (1.0 + m[1:2]) + m[0:1]).astype(BF16)
    for j in range(5):
        z = jnp.dot(h, w_ref[:, j * C_WIDTH:(j + 1) * C_WIDTH], preferred_element_type=F32)
        if j < 3:
            qig_ref[:, j * C_WIDTH:(j + 1) * C_WIDTH] = z.astype(BF16)
        else:
            ff_ref[:, (j - 3) * C_WIDTH:(j - 2) * C_WIDTH] = z


def _hgrn_in(x, mod, w_in):
    row = lambda i: (i, 0)
    return pl.pallas_call(
        _hgrn_in_kernel,
        grid=(N_BLOCKS,),
        in_specs=[pl.BlockSpec((TM, D_MODEL), row), _mod_spec(), _resident((D_MODEL, 5 * C_WIDTH))],
        out_specs=[pl.BlockSpec((TM, QIG_WIDTH), row), pl.BlockSpec((TM, FF_WIDTH), row)],
        out_shape=[jax.ShapeDtypeStruct((N_TOK, QIG_WIDTH), BF16), jax.ShapeDtypeStruct((N_TOK, FF_WIDTH), F32)],
        compiler_params=_params("arbitrary"),
        name="hgrn_in",
    )(x, mod, w_in.astype(BF16))


SCAN_LEVELS = tuple(1 << j for j in range(SCAN_CHUNK.bit_length() - 1))
FINE_LEVELS = tuple(h for h in SCAN_LEVELS if h < SUBLANES)
N_LEVELS = len(SCAN_LEVELS)
DIAG_CODE = 2 * N_LEVELS


def _is_query_side(t, half, forward):
    right = (t & half) != 0
    return right if forward else ~right


def _decay_matrix(forward):
    idx = np.arange(SCAN_CHUNK)
    t, s = idx[:, None], idx[None, :]
    cum = (s <= t) if forward else (s >= t)
    mats = [cum.astype(np.float32)]
    for half in FINE_LEVELS:
        ref = (idx // (2 * half)) * (2 * half) + (half - 1 if forward else half)
        sign = np.where(_is_query_side(idx, half, forward), 1.0, -1.0)[:, None]
        mats.append(sign * (cum.astype(np.float32) - cum[ref].astype(np.float32)))
    mat = np.concatenate(mats, axis=0)
    return jnp.asarray(np.concatenate([mat, mat], axis=1), dtype=BF16)


def _pair_codes():
    idx = np.arange(SCAN_CHUNK)
    t, s = idx[:, None], idx[None, :]
    lev = np.floor(np.log2(np.maximum(t ^ s, 1))).astype(np.int32)
    code = np.where(t > s, lev, np.where(t < s, N_LEVELS + lev, DIAG_CODE))
    return jnp.asarray(code, dtype=jnp.int32)


def _decay_stage(f_raw, lb, dmat2_ref):
    f = lb + (1.0 - lb) * _sigmoid(f_raw)
    g = jnp.log2(f)
    hi = g.astype(BF16)
    lo = (g - hi.astype(F32)).astype(BF16)
    return jnp.dot(dmat2_ref[...], jnp.concatenate([hi, lo], axis=0), preferred_element_type=F32), 1.0 - f


def _fine_levels(q, k, expo, code, forward, scores):
    C = SCAN_CHUNK
    t = lax.broadcasted_iota(jnp.int32, (C, 1), 0)
    base = 0 if forward else N_LEVELS
    for j, half in enumerate(FINE_LEVELS):
        w = jnp.exp2(expo[(j + 1) * C:(j + 2) * C])
        x = (jnp.where(_is_query_side(t, half, forward), q, k) * w).astype(BF16)
        prod = lax.dot_general(x, x, NT_DIMS, preferred_element_type=F32)
        scores = jnp.where(code == base + j, prod, scores)
    return scores


def _coarse_levels(q, k, cum, code, forward, rows):
    C = SCAN_CHUNK
    base = 0 if forward else N_LEVELS
    for j, half in enumerate(SCAN_LEVELS):
        if half in FINE_LEVELS:
            continue
        blk = 2 * half
        xq, xk, q_rows = [], [], []
        zero = jnp.zeros((half, C_DK), F32)
        for n in range(C // blk):
            lo, hi = slice(n * blk, n * blk + half), slice(n * blk + half, (n + 1) * blk)
            if forward:
                ref = cum[n * blk + half - 1:n * blk + half]
                xk += [k[lo] * jnp.exp2(ref - cum[lo]), zero]
                xq.append(q[hi] * jnp.exp2(cum[hi] - ref))
                q_rows.append(n * blk + half)
            else:
                ref = cum[n * blk + half:n * blk + half + 1]
                xq.append(q[lo] * jnp.exp2(cum[lo] - ref))
                xk += [zero, k[hi] * jnp.exp2(ref - cum[hi])]
                q_rows.append(n * blk)
        prod = lax.dot_general(jnp.concatenate(xq, axis=0).astype(BF16), jnp.concatenate(xk, axis=0).astype(BF16),
                               NT_DIMS, preferred_element_type=F32)
        for n, r0 in enumerate(q_rows):
            for i in range(half // SUBLANES):
                rb, src = r0 // SUBLANES + i, n * half + SUBLANES * i
                own = code[SUBLANES * rb:SUBLANES * (rb + 1)] == base + j
                rows[rb] = jnp.where(own, prod[src:src + SUBLANES], rows[rb])
    return rows


def _hgrn_scan_kernel(*refs, sub_len, has_init, aliased):
    q_ref, v_ref, ff_ref, fb_ref, lbl_ref, dmf_ref, dmb_ref, code_ref = refs[:8]
    refs = refs[8:]
    if has_init:
        s0_ref, refs = refs[0], refs[1:]
    if aliased:
        refs = refs[1:]
    o_ref, refs = refs[0], refs[1:]
    if not has_init:
        sout_ref, refs = refs[0], refs[1:]
    (sc_ref, qdf_ref, qdb_ref, decf_ref, decb_ref, incf_ref, incb_ref, stf_ref, stb_ref,
     expo_ref, k_ref, kd_ref) = refs
    C = SCAN_CHUNK
    n_chunks = SCAN_BLOCK // C
    sub_chunks = sub_len // C

    lg = lbl_ref[...]
    ex = jnp.exp(lg - jnp.max(lg, axis=0, keepdims=True))
    prob = ex / jnp.sum(ex, axis=0, keepdims=True)
    lb = jnp.sum(prob[1:DEPTH], axis=0)
    lb_f, lb_b = lb[0:1], lb[1:2]
    code = code_ref[...]

    def rows_of(c):
        return pl.ds(pl.multiple_of(c * C, C), C)

    def local_trip(i, slot):
        c3 = jnp.clip(i - 2, 0, n_chunks - 1)
        v3 = v_ref[rows_of(c3), :]
        incf_ref[c3] = lax.dot_general(v3, kd_ref[slot, 0], TN_DIMS, preferred_element_type=F32)
        incb_ref[c3] = lax.dot_general(v3, kd_ref[slot, 1], TN_DIMS, preferred_element_type=F32)
        c2 = jnp.clip(i - 1, 0, n_chunks - 1)
        r2 = rows_of(c2)
        q = q_ref[r2, :].astype(F32)
        expo_f, expo_b = expo_ref[1 - slot, 0], expo_ref[1 - slot, 1]
        k_f, k_b = k_ref[1 - slot, 0], k_ref[1 - slot, 1]
        cum_f, cum_b = expo_f[:C], expo_b[:C]
        scores = _fine_levels(q, k_f, expo_f, code, True, jnp.zeros((C, C), F32))
        scores = _fine_levels(q, k_b, expo_b, code, False, scores)
        rows = [scores[SUBLANES * n:SUBLANES * (n + 1)] for n in range(C // SUBLANES)]
        rows = _coarse_levels(q, k_f, cum_f, code, True, rows)
        rows = _coarse_levels(q, k_b, cum_b, code, False, rows)
        same_row = jnp.sum(q * (k_f + k_b), axis=-1, keepdims=True)
        sc_ref[c2] = jnp.where(code == DIAG_CODE, same_row, jnp.concatenate(rows, axis=0)).astype(BF16)
        last_f, last_b = cum_f[C - 1:C, :], cum_b[0:1, :]
        qdf_ref[r2, :] = (q * jnp.exp2(cum_f)).astype(BF16)
        qdb_ref[r2, :] = (q * jnp.exp2(cum_b)).astype(BF16)
        decf_ref[pl.ds(c2, 1), :] = jnp.exp2(last_f)
        decb_ref[pl.ds(c2, 1), :] = jnp.exp2(last_b)
        kd_ref[1 - slot, 0] = (k_f * jnp.exp2(last_f - cum_f)).astype(BF16)
        kd_ref[1 - slot, 1] = (k_b * jnp.exp2(last_b - cum_b)).astype(BF16)
        decays(jnp.minimum(i, n_chunks - 1), slot)

    def decays(c, slot):
        r1 = rows_of(c)
        expo_ref[slot, 0], k_ref[slot, 0] = _decay_stage(ff_ref[r1, :], lb_f, dmf_ref)
        expo_ref[slot, 1], k_ref[slot, 1] = _decay_stage(fb_ref[r1, :], lb_b, dmb_ref)

    decays(0, 0)
    kd_ref[1] = jnp.zeros(kd_ref.shape[1:], BF16)

    def local_step(j, carry):
        local_trip(1 + 2 * j, 1)
        local_trip(2 + 2 * j, 0)
        return carry

    lax.fori_loop(0, (n_chunks + 2) // 2, local_step, 0)

    for sub in range(SCAN_BLOCK // sub_len):
        first = sub * sub_chunks
        if has_init:
            init = (s0_ref[0].T, s0_ref[1].T)
        else:
            init = (jnp.zeros((C_DV, C_DK), F32),) * 2

        def state_step(c, carry):
            st_f, st_b = carry
            cf, cb = first + c, first + sub_chunks - 1 - c
            stf_ref[cf] = st_f.astype(BF16)
            stb_ref[cb] = st_b.astype(BF16)
            return (st_f * decf_ref[pl.ds(cf, 1), :] + incf_ref[cf], st_b * decb_ref[pl.ds(cb, 1), :] + incb_ref[cb])

        st_f, st_b = lax.fori_loop(0, sub_chunks, state_step, init, unroll=2)
        if not has_init:
            sout_ref[sub, 0] = st_f.T
            sout_ref[sub, 1] = st_b.T

    def out_step(c, carry):
        r = rows_of(c)
        o_ref[r, :] = (jnp.dot(sc_ref[c], v_ref[r, :], preferred_element_type=F32)
                       + lax.dot_general(qdf_ref[r, :], stf_ref[c], NT_DIMS, preferred_element_type=F32)
                       + lax.dot_general(qdb_ref[r, :], stb_ref[c], NT_DIMS, preferred_element_type=F32))
        return carry

    lax.fori_loop(0, n_chunks, out_step, 0, unroll=2)


def _hgrn_scan(qig, ff, lb_logits, sub_len, row_off, n_rows, s0, o_prev):
    n_blocks = n_rows // SCAN_BLOCK
    blk_off = row_off // SCAN_BLOCK
    n_sub = SCAN_BLOCK // sub_len
    heads = C_WIDTH // C_DK
    n_chunks = SCAN_BLOCK // SCAN_CHUNK
    col = lambda part: pl.BlockSpec((SCAN_BLOCK, C_DK), lambda n, h: (blk_off + n, part * heads + h))
    dmat_rows = (1 + len(FINE_LEVELS)) * SCAN_CHUNK
    in_specs = [col(0), col(1), col(0), col(1),
                pl.BlockSpec((DEPTH, 2, C_DK), lambda n, h: (0, 0, h)),
                pl.BlockSpec((dmat_rows, 2 * SCAN_CHUNK), lambda n, h: (0, 0)),
                pl.BlockSpec((dmat_rows, 2 * SCAN_CHUNK), lambda n, h: (0, 0)),
                pl.BlockSpec((SCAN_CHUNK, SCAN_CHUNK), lambda n, h: (0, 0))]
    args = [qig, qig, ff, ff, lb_logits, _decay_matrix(True), _decay_matrix(False), _pair_codes()]
    state_spec = lambda n_seq: pl.BlockSpec((n_seq, 2, None, C_DK, C_DV), lambda n, h: (n, 0, h, 0, 0))
    if s0 is not None:
        assert n_sub == 1
        in_specs.append(pl.BlockSpec((None, 2, None, C_DK, C_DV), lambda n, h: (n, 0, h, 0, 0)))
        args.append(s0)
    aliases = {}
    if o_prev is not None:
        aliases = {len(args): 0}
        in_specs.append(pl.BlockSpec(memory_space=pl.ANY))
        args.append(o_prev)
    out_specs = [pl.BlockSpec((SCAN_BLOCK, C_DV), lambda n, h: (blk_off + n, h))]
    out_shape = [jax.ShapeDtypeStruct((N_TOK, C_WIDTH), F32)]
    if s0 is None:
        out_specs.append(state_spec(n_sub))
        out_shape.append(jax.ShapeDtypeStruct((n_blocks * n_sub, 2, heads, C_DK, C_DV), F32))
    chunk_mats = lambda dt: pltpu.VMEM((n_chunks, SCAN_CHUNK, SCAN_CHUNK), dt)
    return pl.pallas_call(
        functools.partial(_hgrn_scan_kernel, sub_len=sub_len, has_init=s0 is not None, aliased=o_prev is not None),
        grid=(n_blocks, heads),
        in_specs=in_specs, out_specs=out_specs, out_shape=out_shape,
        input_output_aliases=aliases,
        scratch_shapes=[chunk_mats(BF16), pltpu.VMEM((SCAN_BLOCK, C_DK), BF16), pltpu.VMEM((SCAN_BLOCK, C_DK), BF16),
                        pltpu.VMEM((n_chunks, C_DK), F32), pltpu.VMEM((n_chunks, C_DK), F32),
                        chunk_mats(F32), chunk_mats(F32), chunk_mats(BF16), chunk_mats(BF16),
                        pltpu.VMEM((2, 2, dmat_rows, C_DK), F32), pltpu.VMEM((2, 2, SCAN_CHUNK, C_DK), F32),
                        pltpu.VMEM((2, 2, SCAN_CHUNK, C_DK), BF16)],
        compiler_params=_params("arbitrary", "arbitrary"),
        name="hgrn_scan_init" if s0 is not None else "hgrn_scan_zero",
    )(*args)


def _hgrn_out_kernel(x_ref, mod_ref, o_ref, gate_ref, ng_ref, gmat_ref, w_ref, g_ref, b_ref, out_ref):
    m = mod_ref[0]
    o = o_ref[...]
    o = o * lax.rsqrt(_group_mean_square(o, gmat_ref) + EPS) * ng_ref[...] * _silu(gate_ref[...].astype(F32))
    y = jnp.dot(o.astype(BF16), w_ref[...], preferred_element_type=F32)
    out_ref[...] = _residual_norm(x_ref[...], y, m[2:3], g_ref[...], b_ref[...])


def _hgrn_out(x, mod, o, qig, norm_g, w_out, g, b):
    row = lambda i: (i, 0)
    return pl.pallas_call(
        _hgrn_out_kernel,
        grid=(N_BLOCKS,),
        in_specs=[pl.BlockSpec((TM, D_MODEL), row), _mod_spec(),
                  pl.BlockSpec((TM, C_WIDTH), row), pl.BlockSpec((TM, C_WIDTH), lambda i: (i, 2)),
                  _resident((1, C_WIDTH)), _resident((256, 256)), _resident((C_WIDTH, D_MODEL)),
                  _resident((1, D_MODEL)), _resident((1, D_MODEL))],
        out_specs=pl.BlockSpec((TM, D_MODEL), row),
        out_shape=jax.ShapeDtypeStruct((N_TOK, D_MODEL), F32),
        compiler_params=_params("arbitrary"),
        name="hgrn_out",
    )(x, mod, o, qig, jnp.tile(norm_g, C_HEADS)[None], _group_mean_matrix(C_DV), w_out.astype(BF16), g[None], b[None])


def kernel(x_prompt, x_sample, cache_k, cache_v, state_hgrn, c, c_ctx, w_mod, b_mod, ln1_g, ln1_b, ln2_g, ln2_b,
           attn_w_in, attn_q_gain, attn_k_gain, sconv_w, attn_w_out, hgrn_w_in, hgrn_lb_logits, hgrn_norm_g,
           hgrn_w_out, ffn_w_up, ffn_conv_w, ffn_w_down):
    x_p = x_prompt.reshape(P_TOK, D_MODEL)
    x_s = x_sample.reshape(S_TOK, D_MODEL)
    cond = jnp.concatenate([c_ctx[None], c, jnp.zeros((MOD_ROWS - 1 - DEC_BATCH, D_MODEL), F32)], axis=0)
    mod = _modulation(cond, w_mod, b_mod).reshape(DEPTH, MOD_ROWS, 6, D_MODEL)

    q, k, v, bg, u = _attn_in(x_p, x_s, mod[0], attn_w_in[0], attn_q_gain[0], attn_k_gain[0])
    att = _attend_prompt(q, k, v)
    att = _attend_latent(q, k, v, cache_k[:, 0].reshape(DEC_BATCH, PAST_LEN, KV_WIDTH),
                         cache_v[:, 0].reshape(DEC_BATCH, PAST_LEN, KV_WIDTH), att)
    x = _attn_out(x_p, x_s, mod[0], att, bg, u, sconv_w[0], attn_w_out[0], ln1_g[0], ln1_b[0])
    x = _ffn(x, mod[0], ffn_w_up[0], ffn_conv_w[0], ffn_w_down[0], ln2_g[0], ln2_b[0], split_out=False)
    new_k = k[:P_TOK].reshape(BATCH, 1, SEQ, N_KV, HEAD_DIM)
    new_v = v[:P_TOK].reshape(BATCH, 1, SEQ, N_KV, HEAD_DIM)

    qig, ff = _hgrn_in(x, mod[1], hgrn_w_in[0])
    o, s_new = _hgrn_scan(qig, ff, hgrn_lb_logits, SEQ, 0, P_TOK, None, None)
    o, = _hgrn_scan(qig, ff, hgrn_lb_logits, DEC_SEQ, P_TOK, S_TOK, state_hgrn[:, 0], o)
    x = _hgrn_out(x, mod[1], o, qig, hgrn_norm_g[0], hgrn_w_out[0], ln1_g[1], ln1_b[1])
    y_p, y_s = _ffn(x, mod[1], ffn_w_up[1], ffn_conv_w[1], ffn_w_down[1], ln2_g[1], ln2_b[1], split_out=True)

    return (y_p.reshape(BATCH, SEQ, D_MODEL), y_s.reshape(DEC_BATCH, DEC_SEQ, D_MODEL), new_k, new_v, s_new[:, None])
```

```python
import functools

import jax
import jax.numpy as jnp
import numpy as np
from jax import lax
from jax.experimental import pallas as pl
from jax.experimental.pallas import tpu as pltpu

F32 = jnp.float32
BF16 = jnp.bfloat16

D_MODEL = 1024
BATCH = 32
SEQ = 256
DEPTH = 2
DEC_BATCH = 4
DEC_SEQ = 2048
PAST_LEN = 512
GRID_W = 64
N_HEADS = 8
N_KV = 2
HEAD_DIM = 64
Q_GROUP = N_HEADS // N_KV
ATT_WIDTH = N_HEADS * HEAD_DIM
KV_WIDTH = N_KV * HEAD_DIM
ROPE_AXIS_DIM = HEAD_DIM // 2
ROPE_THETA = 10000.0
SC_WIDTH = D_MODEL - ATT_WIDTH
CONV_WIDTH = 3
C_HEADS = 8
C_DK = D_MODEL // C_HEADS
C_DV = D_MODEL // C_HEADS
C_WIDTH = C_HEADS * C_DK
D_FF = 2816
EVEN_IN_WIDTH = ATT_WIDTH + 2 * KV_WIDTH + 3 * SC_WIDTH
ALPHA = (2 * DEPTH) ** 0.25
EPS = 1e-6

P_TOK = BATCH * SEQ
S_TOK = DEC_BATCH * DEC_SEQ
N_TOK = P_TOK + S_TOK
MOD_ROWS = 8
SUBLANES = 8
TM = 512
HALO = SUBLANES
FF_CHUNK = 256
SCAN_CHUNK = 128
SCAN_BLOCK = DEC_SEQ
TQ = 256
VMEM_LIMIT = 56 * 1024 * 1024

assert P_TOK % TM == 0 and S_TOK % TM == 0 and DEC_SEQ % TM == 0 and TM % SEQ == 0 and DEC_SEQ % SEQ == 0
assert D_FF % FF_CHUNK == 0 and SEQ % SCAN_CHUNK == 0 and DEC_SEQ % SCAN_CHUNK == 0
assert SCAN_BLOCK % SEQ == 0 and P_TOK % SCAN_BLOCK == 0
P_BLOCKS = P_TOK // TM
N_BLOCKS = N_TOK // TM
S_BLOCKS_PER_SEQ = DEC_SEQ // TM

NT_DIMS = (((1,), (1,)), ((), ()))
TN_DIMS = (((0,), (0,)), ((), ()))


def _params(*sem):
    return pltpu.CompilerParams(dimension_semantics=sem, vmem_limit_bytes=VMEM_LIMIT)


def _mod_row(i):
    return jnp.where(i < P_BLOCKS, 0, 1 + (i - P_BLOCKS) // S_BLOCKS_PER_SEQ)


def _mod_spec():
    return pl.BlockSpec((1, 6, D_MODEL), lambda i: (_mod_row(i), 0, 0))


def _resident(shape):
    nd = len(shape)
    return pl.BlockSpec(shape, lambda *_: (0,) * nd, pipeline_mode=pl.Buffered(1))


def _split_specs(width):
    return (pl.BlockSpec((TM, width), lambda i: (jnp.minimum(i, P_BLOCKS - 1), 0)),
            pl.BlockSpec((TM, width), lambda i: (jnp.maximum(i - P_BLOCKS, 0), 0)))


def _sigmoid(x):
    return 0.5 * jnp.tanh(0.5 * x) + 0.5


def _silu(x):
    hx = 0.5 * x
    return hx * jnp.tanh(hx) + hx


def _layer_norm(x, g, b):
    mu = jnp.mean(x, axis=-1, keepdims=True)
    xc = x - mu
    var = jnp.mean(xc * xc, axis=-1, keepdims=True)
    return xc * lax.rsqrt(var + EPS) * g + b


def _residual_norm(x, y, gate, g, b):
    return _layer_norm(ALPHA * x + gate * y, g, b)


def _group_mean_matrix(group):
    idx = np.arange(256)
    return jnp.asarray((idx[:, None] // group == idx[None, :] // group) / group, dtype=BF16)


def _group_mean_square(x, gmat_ref):
    sq = (x * x).astype(BF16)
    n = x.shape[1] // 256
    parts = [jnp.dot(sq[:, 256 * j:256 * (j + 1)], gmat_ref[...], preferred_element_type=F32) for j in range(n)]
    return parts[0] if n == 1 else jnp.concatenate(parts, axis=1)


def _seq_pos(i, rows, first_row):
    seq_len = jnp.where(i < P_BLOCKS, SEQ, DEC_SEQ)
    r = lax.broadcasted_iota(jnp.int32, (rows, 1), 0) + (i * TM + first_row)
    return r & (seq_len - 1), seq_len


def _scale_tiles(x, factor, tiles):
    parts, pos = [], 0
    for t in sorted(tiles):
        lo = t * SUBLANES
        if lo > pos:
            parts.append(x[pos:lo])
        parts.append(x[lo:lo + SUBLANES] * factor[lo:lo + SUBLANES])
        pos = lo + SUBLANES
    if pos < x.shape[0]:
        parts.append(x[pos:])
    return jnp.concatenate(parts, axis=0)


MOD_TN = 1536


def _mod_kernel(cond_ref, w_ref, b_ref, o_ref):
    s = _silu(cond_ref[...])
    o_ref[...] = jnp.dot(s, w_ref[...], preferred_element_type=F32) + b_ref[...]


def _modulation(cond, w_mod, b_mod):
    n_out = 6 * D_MODEL
    return pl.pallas_call(
        _mod_kernel,
        grid=(DEPTH, n_out // MOD_TN),
        in_specs=[pl.BlockSpec((MOD_ROWS, D_MODEL), lambda l, j: (0, 0)),
                  pl.BlockSpec((None, D_MODEL, MOD_TN), lambda l, j: (l, 0, j)),
                  pl.BlockSpec((None, 1, MOD_TN), lambda l, j: (l, 0, j))],
        out_specs=pl.BlockSpec((None, MOD_ROWS, MOD_TN), lambda l, j: (l, 0, j)),
        out_shape=jax.ShapeDtypeStruct((DEPTH, MOD_ROWS, n_out), F32),
        compiler_params=_params("arbitrary", "arbitrary"),
        name="modulation",
    )(cond, w_mod, b_mod.reshape(DEPTH, 1, n_out))


def _rope_tables():
    t = np.arange(DEC_SEQ)
    half = ROPE_AXIS_DIM // 2
    inv = (ROPE_THETA ** (-np.arange(0, ROPE_AXIS_DIM, 2, dtype=np.float32) / ROPE_AXIS_DIM)).astype(np.float32)
    row = (t // GRID_W).astype(np.float32)[:, None] * inv
    col = (t % GRID_W).astype(np.float32)[:, None] * inv
    ang = np.concatenate([row, row, col, col], axis=1).astype(np.float32)
    sign = np.concatenate([-np.ones(half), np.ones(half)] * 2).astype(np.float32)
    cos = np.tile(np.cos(ang), (1, 128 // HEAD_DIM))
    sin = np.tile(np.sin(ang) * sign, (1, 128 // HEAD_DIM))
    return jnp.asarray(cos, F32), jnp.asarray(sin, F32)


def _rope(x, cos, sin):
    n = x.shape[1] // 128
    if n > 1:
        cos = jnp.concatenate([cos] * n, axis=1)
        sin = jnp.concatenate([sin] * n, axis=1)
    lane = lax.broadcasted_iota(jnp.int32, x.shape, 1)
    half = ROPE_AXIS_DIM // 2
    partner = jnp.where((lane & (ROPE_AXIS_DIM - 1)) < half,
                        pltpu.roll(x, x.shape[1] - half, 1), pltpu.roll(x, half, 1))
    return x * cos + partner * sin


def _attn_in_kernel(xp_ref, xs_ref, mod_ref, w_ref, qg_ref, kg_ref, gmat_ref, cos_ref, sin_ref,
                    q_ref, k_ref, v_ref, bg_ref, u_ref):
    i = pl.program_id(0)
    m = mod_ref[0]
    x = jnp.where(i < P_BLOCKS, xp_ref[...], xs_ref[...])
    h = (x * (1.0 + m[1:2]) + m[0:1]).astype(BF16)
    z = jnp.dot(h, w_ref[...], preferred_element_type=F32)
    q = z[:, :ATT_WIDTH]
    k = z[:, ATT_WIDTH:ATT_WIDTH + KV_WIDTH]
    o = ATT_WIDTH + 2 * KV_WIDTH
    q = q * lax.rsqrt(_group_mean_square(q, gmat_ref) + EPS) * qg_ref[...]
    kk = (k * k).astype(BF16)
    k_ms = jnp.dot(kk, gmat_ref[:KV_WIDTH, :KV_WIDTH], preferred_element_type=F32)
    k = k * lax.rsqrt(k_ms + EPS) * kg_ref[...]
    v_ref[...] = z[:, ATT_WIDTH + KV_WIDTH:o]
    bg_ref[...] = z[:, o:o + SC_WIDTH].astype(BF16)
    u_ref[...] = z[:, o + SC_WIDTH:o + 2 * SC_WIDTH] * z[:, o + 2 * SC_WIDTH:]

    @pl.when(i < P_BLOCKS)
    def _():
        q_ref[...] = (q * HEAD_DIM ** -0.5).astype(BF16)
        k_ref[...] = k

    @pl.when(i >= P_BLOCKS)
    def _():
        cos, sin = cos_ref[...], sin_ref[...]
        q_ref[...] = (_rope(q, cos, sin) * HEAD_DIM ** -0.5).astype(BF16)
        k_ref[...] = _rope(k, cos, sin)


def _attn_in(x_p, x_s, mod, w_in, q_gain, k_gain):
    cos, sin = _rope_tables()
    row = lambda i: (i, 0)
    rope_row = lambda i: (jnp.maximum(i - P_BLOCKS, 0) % S_BLOCKS_PER_SEQ, 0)
    tok = lambda w, dt: jax.ShapeDtypeStruct((N_TOK, w), dt)
    return pl.pallas_call(
        _attn_in_kernel,
        grid=(N_BLOCKS,),
        in_specs=[*_split_specs(D_MODEL), _mod_spec(),
                  _resident((D_MODEL, EVEN_IN_WIDTH)),
                  _resident((1, ATT_WIDTH)), _resident((1, KV_WIDTH)), _resident((256, 256)),
                  pl.BlockSpec((TM, 128), rope_row), pl.BlockSpec((TM, 128), rope_row)],
        out_specs=[pl.BlockSpec((TM, ATT_WIDTH), row), pl.BlockSpec((TM, KV_WIDTH), row),
                   pl.BlockSpec((TM, KV_WIDTH), row), pl.BlockSpec((TM, SC_WIDTH), row),
                   pl.BlockSpec((TM, SC_WIDTH), row)],
        out_shape=[tok(ATT_WIDTH, BF16), tok(KV_WIDTH, F32), tok(KV_WIDTH, F32), tok(SC_WIDTH, BF16),
                   tok(SC_WIDTH, F32)],
        compiler_params=_params("arbitrary"),
        name="attn_in",
    )(x_p, x_s, mod, w_in.astype(BF16), jnp.tile(q_gain, N_HEADS)[None], jnp.tile(k_gain, N_KV)[None],
      _group_mean_matrix(HEAD_DIM), cos, sin)


def _attend_kernel(q_ref, *refs, n_kv_sets):
    kv_refs, o_ref = refs[:2 * n_kv_sets], refs[-1]
    ks = [r[...].astype(BF16) for r in kv_refs[:n_kv_sets]]
    vs = [r[...].astype(BF16) for r in kv_refs[n_kv_sets:]]
    q = q_ref[...]
    for hd in range(N_HEADS):
        g = hd // Q_GROUP
        qh = q[:, hd * HEAD_DIM:(hd + 1) * HEAD_DIM]
        s = [lax.dot_general(qh, kx[:, g * HEAD_DIM:(g + 1) * HEAD_DIM], NT_DIMS, preferred_element_type=F32)
             for kx in ks]
        mx = functools.reduce(jnp.maximum, [jnp.max(sx, axis=-1, keepdims=True) for sx in s])
        p = [jnp.exp(sx - mx) for sx in s]
        den = functools.reduce(jnp.add, [jnp.sum(px, axis=-1, keepdims=True) for px in p])
        acc = functools.reduce(jnp.add, [
            jnp.dot(px.astype(BF16), vx[:, g * HEAD_DIM:(g + 1) * HEAD_DIM], preferred_element_type=F32)
            for px, vx in zip(p, vs)])
        o_ref[:, hd * HEAD_DIM:(hd + 1) * HEAD_DIM] = (acc / den).astype(BF16)


def _attend_prompt(q, k, v):
    blk = lambda w: pl.BlockSpec((SEQ, w), lambda b: (b, 0))
    return pl.pallas_call(
        functools.partial(_attend_kernel, n_kv_sets=1),
        grid=(BATCH,),
        in_specs=[blk(ATT_WIDTH), blk(KV_WIDTH), blk(KV_WIDTH)],
        out_specs=blk(ATT_WIDTH),
        out_shape=jax.ShapeDtypeStruct((P_TOK, ATT_WIDTH), BF16),
        compiler_params=_params("arbitrary"),
        name="attend_prompt",
    )(q, k, v)


def _attend_latent(q, k, v, ctx_k, ctx_v):
    nq = DEC_SEQ // TQ
    q_off, kv_off = P_TOK // TQ, P_TOK // DEC_SEQ
    lat = pl.BlockSpec((DEC_SEQ, KV_WIDTH), lambda b, j: (kv_off + b, 0))
    ctx = pl.BlockSpec((None, PAST_LEN, KV_WIDTH), lambda b, j: (b, 0, 0))
    return pl.pallas_call(
        functools.partial(_attend_kernel, n_kv_sets=2),
        grid=(DEC_BATCH, nq),
        in_specs=[pl.BlockSpec((TQ, ATT_WIDTH), lambda b, j: (q_off + b * nq + j, 0)), lat, ctx, lat, ctx],
        out_specs=pl.BlockSpec((TQ, ATT_WIDTH), lambda b, j: (b * nq + j, 0)),
        out_shape=jax.ShapeDtypeStruct((S_TOK, ATT_WIDTH), BF16),
        compiler_params=_params("arbitrary", "arbitrary"),
        name="attend_latent",
    )(q, k, ctx_k, v, ctx_v)


def _halo_specs(width):
    per = TM // HALO
    prev = pl.BlockSpec((HALO, width), lambda i: (jnp.maximum(i * per - 1, 0), 0))
    nxt = pl.BlockSpec((HALO, width), lambda i: (jnp.minimum((i + 1) * per, N_TOK // HALO - 1), 0))
    return prev, nxt


def _attn_out_kernel(xp_ref, xs_ref, mod_ref, attp_ref, atts_ref, bg_ref, u_ref, up_ref, un_ref, cw_ref, w_ref,
                     g_ref, b_ref, o_ref):
    i = pl.program_id(0)
    m = mod_ref[0]
    pos, seq_len = _seq_pos(i, TM, 0)
    r = lax.broadcasted_iota(jnp.int32, (TM, 1), 0)
    u = u_ref[...]
    u_prev = jnp.where(r == 0, up_ref[HALO - 1:HALO, :], pltpu.roll(u, 1, 0))
    u_next = jnp.where(r == TM - 1, un_ref[0:1, :], pltpu.roll(u, TM - 1, 0))
    u_prev = jnp.where(pos == 0, 0.0, u_prev)
    u_next = jnp.where(pos == seq_len - 1, 0.0, u_next)
    cw = cw_ref[...]
    sc = bg_ref[...].astype(F32) * (cw[0:1] * u_prev + cw[1:2] * u + cw[2:3] * u_next)
    att = jnp.where(i < P_BLOCKS, attp_ref[...], atts_ref[...])
    y = (jnp.dot(att, w_ref[:ATT_WIDTH, :], preferred_element_type=F32)
         + jnp.dot(sc.astype(BF16), w_ref[ATT_WIDTH:, :], preferred_element_type=F32))
    x = jnp.where(i < P_BLOCKS, xp_ref[...], xs_ref[...])
    o_ref[...] = _residual_norm(x, y, m[2:3], g_ref[...], b_ref[...])


def _attn_out(x_p, x_s, mod, att_p, att_s, bg, u, conv_w, w_out, g, b):
    row = lambda i: (i, 0)
    up, un = _halo_specs(SC_WIDTH)
    return pl.pallas_call(
        _attn_out_kernel,
        grid=(N_BLOCKS,),
        in_specs=[*_split_specs(D_MODEL), _mod_spec(),
                  *_split_specs(ATT_WIDTH), pl.BlockSpec((TM, SC_WIDTH), row),
                  pl.BlockSpec((TM, SC_WIDTH), row), up, un,
                  _resident((CONV_WIDTH, SC_WIDTH)), _resident((D_MODEL, D_MODEL)),
                  _resident((1, D_MODEL)), _resident((1, D_MODEL))],
        out_specs=pl.BlockSpec((TM, D_MODEL), row),
        out_shape=jax.ShapeDtypeStruct((N_TOK, D_MODEL), F32),
        compiler_params=_params("arbitrary"),
        name="attn_out",
    )(x_p, x_s, mod, att_p, att_s, bg, u, u, u, conv_w, w_out.astype(BF16), g[None], b[None])


_FFN_FIRST_TILES = tuple((HALO + n * SEQ) // SUBLANES for n in range(TM // SEQ))
_FFN_LAST_TILES = tuple((HALO + n * SEQ - 1) // SUBLANES for n in range(1, TM // SEQ + 1))


def _ffn_kernel(x_ref, xp_ref, xn_ref, mod_ref, wu_ref, cw_ref, wd_ref, g_ref, b_ref, *refs, split_out):
    o_refs, (h_ref, act_ref), u_refs = refs[:-6], refs[-6:-4], refs[-4:]
    u_ref = lambda slot, part: u_refs[2 * slot + part]
    i = pl.program_id(0)
    m = mod_ref[0]
    rows = TM + 2 * HALO
    scale, shift = 1.0 + m[4:5], m[3:4]
    h_ref[:HALO, :] = (xp_ref[...] * scale + shift).astype(BF16)
    h_ref[HALO:HALO + TM, :] = (x_ref[...] * scale + shift).astype(BF16)
    h_ref[HALO + TM:, :] = (xn_ref[...] * scale + shift).astype(BF16)
    pos, seq_len = _seq_pos(i, rows, -HALO)
    keep_prev = (pos != 0).astype(F32)
    keep_next = (pos != seq_len - 1).astype(F32)
    n_chunks = D_FF // FF_CHUNK

    def cols(c):
        return pl.multiple_of(c * FF_CHUNK, FF_CHUNK), pl.multiple_of(D_FF + c * FF_CHUNK, FF_CHUNK)

    def up(c, slot):
        h = h_ref[...]
        for part, col in enumerate(cols(c)):
            u_ref(slot, part)[...] = jnp.dot(h, wu_ref[:, pl.ds(col, FF_CHUNK)], preferred_element_type=F32)

    def conv(slot, part, col):
        w = cw_ref[:, pl.ds(col, FF_CHUNK)]
        u = u_ref(slot, part)[...]
        u_prev = _scale_tiles(pltpu.roll(u, 1, 0), keep_prev, _FFN_FIRST_TILES)
        u_next = _scale_tiles(pltpu.roll(u, rows - 1, 0), keep_next, _FFN_LAST_TILES)
        return (w[0:1] * u_prev + w[1:2] * u + w[2:3] * u_next)[HALO:HALO + TM]

    def gate(c, slot):
        col_a, col_g = cols(c)
        act_ref[:, pl.ds(col_a, FF_CHUNK)] = (_silu(conv(slot, 1, col_g)) * conv(slot, 0, col_a)).astype(BF16)

    def step(j, carry):
        c = 1 + 2 * j
        for d, slot in ((0, 1), (1, 0)):
            gate(c + d - 1, 1 - slot)
            up(c + d, slot)
        return carry

    assert n_chunks % 2 == 1
    up(0, 0)
    lax.fori_loop(0, (n_chunks - 1) // 2, step, 0)
    gate(n_chunks - 1, 0)
    y = jnp.dot(act_ref[...], wd_ref[...], preferred_element_type=F32)
    y = _residual_norm(x_ref[...], y, m[5:6], g_ref[...], b_ref[...])
    if split_out:
        @pl.when(i < P_BLOCKS)
        def _():
            o_refs[0][...] = y

        @pl.when(i >= P_BLOCKS)
        def _():
            o_refs[1][...] = y
    else:
        o_refs[0][...] = y


def _ffn(x, mod, w_up, conv_w, w_down, g, b, split_out):
    row = lambda i: (i, 0)
    xp, xn = _halo_specs(D_MODEL)
    if split_out:
        out_specs = list(_split_specs(D_MODEL))
        out_shape = [jax.ShapeDtypeStruct((P_TOK, D_MODEL), F32), jax.ShapeDtypeStruct((S_TOK, D_MODEL), F32)]
    else:
        out_specs = pl.BlockSpec((TM, D_MODEL), row)
        out_shape = jax.ShapeDtypeStruct((N_TOK, D_MODEL), F32)
    return pl.pallas_call(
        functools.partial(_ffn_kernel, split_out=split_out),
        grid=(N_BLOCKS,),
        in_specs=[pl.BlockSpec((TM, D_MODEL), row), xp, xn, _mod_spec(),
                  _resident((D_MODEL, 2 * D_FF)), _resident((CONV_WIDTH, 2 * D_FF)), _resident((D_FF, D_MODEL)),
                  _resident((1, D_MODEL)), _resident((1, D_MODEL))],
        out_specs=out_specs,
        out_shape=out_shape,
        scratch_shapes=[pltpu.VMEM((TM + 2 * HALO, D_MODEL), BF16),
                        pltpu.VMEM((TM, D_FF), BF16)] + [pltpu.VMEM((TM + 2 * HALO, FF_CHUNK), F32)] * 4,
        compiler_params=_params("arbitrary"),
        name="conv_ffn",
    )(x, x, x, mod, w_up.astype(BF16), conv_w, w_down.astype(BF16), g[None], b[None])


QIG_WIDTH = 3 * C_WIDTH
FF_WIDTH = 2 * C_WIDTH


def _hgrn_in_kernel(x_ref, mod_ref, w_ref, qig_ref, ff_ref):
    m = mod_ref[0]
    h = (x_ref[...] * (1.0 + m[1:2]) + m[0:1]).astype(BF16)
    for j in range(5):
        z = jnp.dot(h, w_ref[:, j * C_WIDTH:(j + 1) * C_WIDTH], preferred_element_type=F32)
        if j < 3:
            qig_ref[:, j * C_WIDTH:(j + 1) * C_WIDTH] = z.astype(BF16)
        else:
            ff_ref[:, (j - 3) * C_WIDTH:(j - 2) * C_WIDTH] = z


def _hgrn_in(x, mod, w_in):
    row = lambda i: (i, 0)
    return pl.pallas_call(
        _hgrn_in_kernel,
        grid=(N_BLOCKS,),
        in_specs=[pl.BlockSpec((TM, D_MODEL), row), _mod_spec(), _resident((D_MODEL, 5 * C_WIDTH))],
        out_specs=[pl.BlockSpec((TM, QIG_WIDTH), row), pl.BlockSpec((TM, FF_WIDTH), row)],
        out_shape=[jax.ShapeDtypeStruct((N_TOK, QIG_WIDTH), BF16), jax.ShapeDtypeStruct((N_TOK, FF_WIDTH), F32)],
        compiler_params=_params("arbitrary"),
        name="hgrn_in",
    )(x, mod, w_in.astype(BF16))


SCAN_LEVELS = tuple(1 << j for j in range(SCAN_CHUNK.bit_length() - 1))
FINE_LEVELS = tuple(h for h in SCAN_LEVELS if h < SUBLANES)
N_LEVELS = len(SCAN_LEVELS)
DIAG_CODE = 2 * N_LEVELS


def _is_query_side(t, half, forward):
    right = (t & half) != 0
    return right if forward else ~right


def _decay_matrix(forward):
    idx = np.arange(SCAN_CHUNK)
    t, s = idx[:, None], idx[None, :]
    cum = (s <= t) if forward else (s >= t)
    mats = [cum.astype(np.float32)]
    for half in FINE_LEVELS:
        ref = (idx // (2 * half)) * (2 * half) + (half - 1 if forward else half)
        sign = np.where(_is_query_side(idx, half, forward), 1.0, -1.0)[:, None]
        mats.append(sign * (cum.astype(np.float32) - cum[ref].astype(np.float32)))
    mat = np.concatenate(mats, axis=0)
    return jnp.asarray(np.concatenate([mat, mat], axis=1), dtype=BF16)


def _pair_codes():
    idx = np.arange(SCAN_CHUNK)
    t, s = idx[:, None], idx[None, :]
    lev = np.floor(np.log2(np.maximum(t ^ s, 1))).astype(np.int32)
    code = np.where(t > s, lev, np.where(t < s, N_LEVELS + lev, DIAG_CODE))
    return jnp.asarray(code, dtype=jnp.int32)


def _decay_terms(f_raw, lb):
    f = lb + (1.0 - lb) * _sigmoid(f_raw)
    g = jnp.log2(f)
    hi = g.astype(BF16)
    lo = (g - hi.astype(F32)).astype(BF16)
    return jnp.concatenate([hi, lo], axis=0), 1.0 - f


def _fine_levels(q, k, expo, code, forward, scores):
    C = SCAN_CHUNK
    t = lax.broadcasted_iota(jnp.int32, (C, 1), 0)
    base = 0 if forward else N_LEVELS
    for j, half in enumerate(FINE_LEVELS):
        w = jnp.exp2(expo[(j + 1) * C:(j + 2) * C])
        x = (jnp.where(_is_query_side(t, half, forward), q, k) * w).astype(BF16)
        prod = lax.dot_general(x, x, NT_DIMS, preferred_element_type=F32)
        scores = jnp.where(code == base + j, prod, scores)
    return scores


def _coarse_levels(q, k, cum, code, forward, rows):
    C = SCAN_CHUNK
    base = 0 if forward else N_LEVELS
    for j, half in enumerate(SCAN_LEVELS):
        if half in FINE_LEVELS:
            continue
        blk = 2 * half
        xq, xk, q_rows = [], [], []
        zero = jnp.zeros((half, C_DK), F32)
        for n in range(C // blk):
            lo, hi = slice(n * blk, n * blk + half), slice(n * blk + half, (n + 1) * blk)
            if forward:
                ref = cum[n * blk + half - 1:n * blk + half]
                xk += [k[lo] * jnp.exp2(ref - cum[lo]), zero]
                xq.append(q[hi] * jnp.exp2(cum[hi] - ref))
                q_rows.append(n * blk + half)
            else:
                ref = cum[n * blk + half:n * blk + half + 1]
                xq.append(q[lo] * jnp.exp2(cum[lo] - ref))
                xk += [zero, k[hi] * jnp.exp2(ref - cum[hi])]
                q_rows.append(n * blk)
        prod = lax.dot_general(jnp.concatenate(xq, axis=0).astype(BF16), jnp.concatenate(xk, axis=0).astype(BF16),
                               NT_DIMS, preferred_element_type=F32)
        for n, r0 in enumerate(q_rows):
            for i in range(half // SUBLANES):
                rb, src = r0 // SUBLANES + i, n * half + SUBLANES * i
                own = code[SUBLANES * rb:SUBLANES * (rb + 1)] == base + j
                rows[rb] = jnp.where(own, prod[src:src + SUBLANES], rows[rb])
    return rows


def _hgrn_scan_kernel(*refs, sub_len, has_init):
    q_ref, v_ref, ff_ref, fb_ref, lbl_ref, dmf_ref, dmb_ref, code_ref = refs[:8]
    refs = refs[8:]
    if has_init:
        s0_ref, refs = refs[0], refs[1:]
    o_ref, refs = refs[0], refs[1:]
    if not has_init:
        sout_ref, refs = refs[0], refs[1:]
    (sc_ref, qdf_ref, qdb_ref, decf_ref, decb_ref, incf_ref, incb_ref, stf_ref, stb_ref,
     expo_ref, k_ref, kd_ref, g2_ref) = refs
    C = SCAN_CHUNK
    n_chunks = SCAN_BLOCK // C
    sub_chunks = sub_len // C

    lg = lbl_ref[...]
    ex = jnp.exp(lg - jnp.max(lg, axis=0, keepdims=True))
    prob = ex / jnp.sum(ex, axis=0, keepdims=True)
    lb = jnp.sum(prob[1:DEPTH], axis=0)
    lb_f, lb_b = lb[0:1], lb[1:2]
    code = code_ref[...]

    def rows_of(c):
        return pl.ds(pl.multiple_of(c * C, C), C)

    def local_trip(i, slot, final=False):
        c4 = jnp.clip(i - 3, 0, n_chunks - 1)
        v4 = v_ref[rows_of(c4), :]
        incf_ref[c4] = lax.dot_general(v4, kd_ref[1 - slot, 0], TN_DIMS, preferred_element_type=F32)
        incb_ref[c4] = lax.dot_general(v4, kd_ref[1 - slot, 1], TN_DIMS, preferred_element_type=F32)
        if final:
            return
        c2 = jnp.clip(i - 2, 0, n_chunks - 1)
        r2 = rows_of(c2)
        q = q_ref[r2, :].astype(F32)
        expo_f, expo_b = expo_ref[1 - slot, 0], expo_ref[1 - slot, 1]
        k_f, k_b = k_ref[slot, 0], k_ref[slot, 1]
        cum_f, cum_b = expo_f[:C], expo_b[:C]
        scores = _fine_levels(q, k_f, expo_f, code, True, jnp.zeros((C, C), F32))
        scores = _fine_levels(q, k_b, expo_b, code, False, scores)
        rows = [scores[SUBLANES * n:SUBLANES * (n + 1)] for n in range(C // SUBLANES)]
        rows = _coarse_levels(q, k_f, cum_f, code, True, rows)
        rows = _coarse_levels(q, k_b, cum_b, code, False, rows)
        same_row = jnp.sum(q * (k_f + k_b), axis=-1, keepdims=True)
        sc_ref[c2] = jnp.where(code == DIAG_CODE, same_row, jnp.concatenate(rows, axis=0)).astype(BF16)
        last_f, last_b = cum_f[C - 1:C, :], cum_b[0:1, :]
        qdf_ref[r2, :] = (q * jnp.exp2(cum_f)).astype(BF16)
        qdb_ref[r2, :] = (q * jnp.exp2(cum_b)).astype(BF16)
        decf_ref[pl.ds(c2, 1), :] = jnp.exp2(last_f)
        decb_ref[pl.ds(c2, 1), :] = jnp.exp2(last_b)
        kd_ref[slot, 0] = (k_f * jnp.exp2(last_f - cum_f)).astype(BF16)
        kd_ref[slot, 1] = (k_b * jnp.exp2(last_b - cum_b)).astype(BF16)
        cumulate(slot)
        decays(jnp.minimum(i, n_chunks - 1), slot)

    def cumulate(slot):
        expo_ref[slot, 0] = jnp.dot(dmf_ref[...], g2_ref[1 - slot, 0], preferred_element_type=F32)
        expo_ref[slot, 1] = jnp.dot(dmb_ref[...], g2_ref[1 - slot, 1], preferred_element_type=F32)

    def decays(c, slot):
        r1 = rows_of(c)
        g2_ref[slot, 0], k_ref[slot, 0] = _decay_terms(ff_ref[r1, :], lb_f)
        g2_ref[slot, 1], k_ref[slot, 1] = _decay_terms(fb_ref[r1, :], lb_b)

    decays(0, 0)
    decays(1, 1)
    cumulate(1)
    kd_ref[1] = jnp.zeros(kd_ref.shape[1:], BF16)

    def local_step(j, carry):
        local_trip(2 + 2 * j, 0)
        local_trip(3 + 2 * j, 1)
        return carry

    assert n_chunks % 2 == 0
    lax.fori_loop(0, n_chunks // 2, local_step, 0)
    local_trip(n_chunks + 2, 0, final=True)

    for sub in range(SCAN_BLOCK // sub_len):
        first = sub * sub_chunks
        if has_init:
            init = (s0_ref[0].T, s0_ref[1].T)
        else:
            init = (jnp.zeros((C_DV, C_DK), F32),) * 2

        def state_step(c, carry):
            st_f, st_b = carry
            cf, cb = first + c, first + sub_chunks - 1 - c
            stf_ref[cf] = st_f.astype(BF16)
            stb_ref[cb] = st_b.astype(BF16)
            return (st_f * decf_ref[pl.ds(cf, 1), :] + incf_ref[cf], st_b * decb_ref[pl.ds(cb, 1), :] + incb_ref[cb])

        st_f, st_b = lax.fori_loop(0, sub_chunks, state_step, init, unroll=2)
        if not has_init:
            sout_ref[sub, 0] = st_f.T
            sout_ref[sub, 1] = st_b.T

    def out_step(c, carry):
        r = rows_of(c)
        o_ref[r, :] = (jnp.dot(sc_ref[c], v_ref[r, :], preferred_element_type=F32)
                       + lax.dot_general(qdf_ref[r, :], stf_ref[c], NT_DIMS, preferred_element_type=F32)
                       + lax.dot_general(qdb_ref[r, :], stb_ref[c], NT_DIMS, preferred_element_type=F32))
        return carry

    lax.fori_loop(0, n_chunks, out_step, 0, unroll=2)


def _hgrn_scan(qig, ff, lb_logits, sub_len, row_off, n_rows, s0):
    n_blocks = n_rows // SCAN_BLOCK
    blk_off = row_off // SCAN_BLOCK
    n_sub = SCAN_BLOCK // sub_len
    heads = C_WIDTH // C_DK
    n_chunks = SCAN_BLOCK // SCAN_CHUNK
    col = lambda part: pl.BlockSpec((SCAN_BLOCK, C_DK), lambda n, h: (blk_off + n, part * heads + h))
    dmat_rows = (1 + len(FINE_LEVELS)) * SCAN_CHUNK
    in_specs = [col(0), col(1), col(0), col(1),
                pl.BlockSpec((DEPTH, 2, C_DK), lambda n, h: (0, 0, h)),
                pl.BlockSpec((dmat_rows, 2 * SCAN_CHUNK), lambda n, h: (0, 0)),
                pl.BlockSpec((dmat_rows, 2 * SCAN_CHUNK), lambda n, h: (0, 0)),
                pl.BlockSpec((SCAN_CHUNK, SCAN_CHUNK), lambda n, h: (0, 0))]
    args = [qig, qig, ff, ff, lb_logits, _decay_matrix(True), _decay_matrix(False), _pair_codes()]
    state_spec = lambda n_seq: pl.BlockSpec((n_seq, 2, None, C_DK, C_DV), lambda n, h: (n, 0, h, 0, 0))
    if s0 is not None:
        assert n_sub == 1
        in_specs.append(pl.BlockSpec((None, 2, None, C_DK, C_DV), lambda n, h: (n, 0, h, 0, 0)))
        args.append(s0)
    out_specs = [pl.BlockSpec((SCAN_BLOCK, C_DV), lambda n, h: (n, h))]
    out_shape = [jax.ShapeDtypeStruct((n_rows, C_WIDTH), F32)]
    if s0 is None:
        out_specs.append(state_spec(n_sub))
        out_shape.append(jax.ShapeDtypeStruct((n_blocks * n_sub, 2, heads, C_DK, C_DV), F32))
    chunk_mats = lambda dt: pltpu.VMEM((n_chunks, SCAN_CHUNK, SCAN_CHUNK), dt)
    return pl.pallas_call(
        functools.partial(_hgrn_scan_kernel, sub_len=sub_len, has_init=s0 is not None),
        grid=(n_blocks, heads),
        in_specs=in_specs, out_specs=out_specs, out_shape=out_shape,
        scratch_shapes=[chunk_mats(BF16), pltpu.VMEM((SCAN_BLOCK, C_DK), BF16), pltpu.VMEM((SCAN_BLOCK, C_DK), BF16),
                        pltpu.VMEM((n_chunks, C_DK), F32), pltpu.VMEM((n_chunks, C_DK), F32),
                        chunk_mats(F32), chunk_mats(F32), chunk_mats(BF16), chunk_mats(BF16),
                        pltpu.VMEM((2, 2, dmat_rows, C_DK), F32), pltpu.VMEM((2, 2, SCAN_CHUNK, C_DK), F32),
                        pltpu.VMEM((2, 2, SCAN_CHUNK, C_DK), BF16), pltpu.VMEM((2, 2, 2 * SCAN_CHUNK, C_DK), BF16)],
        compiler_params=_params("arbitrary", "arbitrary"),
        name="hgrn_scan_init" if s0 is not None else "hgrn_scan_zero",
    )(*args)


def _hgrn_out_kernel(x_ref, mod_ref, op_ref, os_ref, gate_ref, ng_ref, gmat_ref, w_ref, g_ref, b_ref, out_ref):
    m = mod_ref[0]
    o = jnp.where(pl.program_id(0) < P_BLOCKS, op_ref[...], os_ref[...])
    o = o * lax.rsqrt(_group_mean_square(o, gmat_ref) + EPS) * ng_ref[...] * _silu(gate_ref[...].astype(F32))
    y = jnp.dot(o.astype(BF16), w_ref[...], preferred_element_type=F32)
    out_ref[...] = _residual_norm(x_ref[...], y, m[2:3], g_ref[...], b_ref[...])


def _hgrn_out(x, mod, o_p, o_s, qig, norm_g, w_out, g, b):
    row = lambda i: (i, 0)
    return pl.pallas_call(
        _hgrn_out_kernel,
        grid=(N_BLOCKS,),
        in_specs=[pl.BlockSpec((TM, D_MODEL), row), _mod_spec(),
                  *_split_specs(C_WIDTH), pl.BlockSpec((TM, C_WIDTH), lambda i: (i, 2)),
                  _resident((1, C_WIDTH)), _resident((256, 256)), _resident((C_WIDTH, D_MODEL)),
                  _resident((1, D_MODEL)), _resident((1, D_MODEL))],
        out_specs=pl.BlockSpec((TM, D_MODEL), row),
        out_shape=jax.ShapeDtypeStruct((N_TOK, D_MODEL), F32),
        compiler_params=_params("arbitrary"),
        name="hgrn_out",
    )(x, mod, o_p, o_s, qig, jnp.tile(norm_g, C_HEADS)[None], _group_mean_matrix(C_DV), w_out.astype(BF16), g[None],
      b[None])


def kernel(x_prompt, x_sample, cache_k, cache_v, state_hgrn, c, c_ctx, w_mod, b_mod, ln1_g, ln1_b, ln2_g, ln2_b,
           attn_w_in, attn_q_gain, attn_k_gain, sconv_w, attn_w_out, hgrn_w_in, hgrn_lb_logits, hgrn_norm_g,
           hgrn_w_out, ffn_w_up, ffn_conv_w, ffn_w_down):
    x_p = x_prompt.reshape(P_TOK, D_MODEL)
    x_s = x_sample.reshape(S_TOK, D_MODEL)
    cond = jnp.concatenate([c_ctx[None], c, jnp.zeros((MOD_ROWS - 1 - DEC_BATCH, D_MODEL), F32)], axis=0)
    mod = _modulation(cond, w_mod, b_mod).reshape(DEPTH, MOD_ROWS, 6, D_MODEL)

    q, k, v, bg, u = _attn_in(x_p, x_s, mod[0], attn_w_in[0], attn_q_gain[0], attn_k_gain[0])
    att_p = _attend_prompt(q, k, v)
    att_s = _attend_latent(q, k, v, cache_k[:, 0].reshape(DEC_BATCH, PAST_LEN, KV_WIDTH),
                           cache_v[:, 0].reshape(DEC_BATCH, PAST_LEN, KV_WIDTH))
    x = _attn_out(x_p, x_s, mod[0], att_p, att_s, bg, u, sconv_w[0], attn_w_out[0], ln1_g[0], ln1_b[0])
    x = _ffn(x, mod[0], ffn_w_up[0], ffn_conv_w[0], ffn_w_down[0], ln2_g[0], ln2_b[0], split_out=False)
    new_k = k[:P_TOK].reshape(BATCH, 1, SEQ, N_KV, HEAD_DIM)
    new_v = v[:P_TOK].reshape(BATCH, 1, SEQ, N_KV, HEAD_DIM)

    qig, ff = _hgrn_in(x, mod[1], hgrn_w_in[0])
    o_p, s_new = _hgrn_scan(qig, ff, hgrn_lb_logits, SEQ, 0, P_TOK, None)
    o_s, = _hgrn_scan(qig, ff, hgrn_lb_logits, DEC_SEQ, P_TOK, S_TOK, state_hgrn[:, 0])
    x = _hgrn_out(x, mod[1], o_p, o_s, qig, hgrn_norm_g[0], hgrn_w_out[0], ln1_g[1], ln1_b[1])
    y_p, y_s = _ffn(x, mod[1], ffn_w_up[1], ffn_conv_w[1], ffn_w_down[1], ln2_g[1], ln2_b[1], split_out=True)

    return (y_p.reshape(BATCH, SEQ, D_MODEL), y_s.reshape(DEC_BATCH, DEC_SEQ, D_MODEL), new_k, new_v, s_new[:, None])
```

```python
import functools

import jax
import jax.numpy as jnp
import numpy as np
from jax import lax
from jax.experimental import pallas as pl
from jax.experimental.pallas import tpu as pltpu

F32 = jnp.float32
BF16 = jnp.bfloat16

D_MODEL = 1024
BATCH = 32
SEQ = 256
DEPTH = 2
DEC_BATCH = 4
DEC_SEQ = 2048
PAST_LEN = 512
GRID_W = 64
N_HEADS = 8
N_KV = 2
HEAD_DIM = 64
Q_GROUP = N_HEADS // N_KV
ATT_WIDTH = N_HEADS * HEAD_DIM
KV_WIDTH = N_KV * HEAD_DIM
ROPE_AXIS_DIM = HEAD_DIM // 2
ROPE_THETA = 10000.0
SC_WIDTH = D_MODEL - ATT_WIDTH
CONV_WIDTH = 3
C_HEADS = 8
C_DK = D_MODEL // C_HEADS
C_DV = D_MODEL // C_HEADS
C_WIDTH = C_HEADS * C_DK
D_FF = 2816
EVEN_IN_WIDTH = ATT_WIDTH + 2 * KV_WIDTH + 3 * SC_WIDTH
ALPHA = (2 * DEPTH) ** 0.25
EPS = 1e-6

P_TOK = BATCH * SEQ
S_TOK = DEC_BATCH * DEC_SEQ
N_TOK = P_TOK + S_TOK
MOD_ROWS = 8
SUBLANES = 8
TM = 512
HALO = SUBLANES
FF_CHUNK = 256
SCAN_CHUNK = 128
SCAN_BLOCK = DEC_SEQ
TQ = 256
VMEM_LIMIT = 56 * 1024 * 1024

assert P_TOK % TM == 0 and S_TOK % TM == 0 and DEC_SEQ % TM == 0 and TM % SEQ == 0 and DEC_SEQ % SEQ == 0
assert D_FF % FF_CHUNK == 0 and SEQ % SCAN_CHUNK == 0 and DEC_SEQ % SCAN_CHUNK == 0
assert SCAN_BLOCK % SEQ == 0 and P_TOK % SCAN_BLOCK == 0
P_BLOCKS = P_TOK // TM
N_BLOCKS = N_TOK // TM
S_BLOCKS_PER_SEQ = DEC_SEQ // TM

NT_DIMS = (((1,), (1,)), ((), ()))
TN_DIMS = (((0,), (0,)), ((), ()))


def _params(*sem):
    return pltpu.CompilerParams(dimension_semantics=sem, vmem_limit_bytes=VMEM_LIMIT)


def _mod_row(i):
    return jnp.where(i < P_BLOCKS, 0, 1 + (i - P_BLOCKS) // S_BLOCKS_PER_SEQ)


def _mod_spec():
    return pl.BlockSpec((1, 6, D_MODEL), lambda i: (_mod_row(i), 0, 0))


def _resident(shape):
    nd = len(shape)
    return pl.BlockSpec(shape, lambda *_: (0,) * nd, pipeline_mode=pl.Buffered(1))


def _split_specs(width):
    return (pl.BlockSpec((TM, width), lambda i: (jnp.minimum(i, P_BLOCKS - 1), 0)),
            pl.BlockSpec((TM, width), lambda i: (jnp.maximum(i - P_BLOCKS, 0), 0)))


def _sigmoid(x):
    return 0.5 * jnp.tanh(0.5 * x) + 0.5


def _silu(x):
    hx = 0.5 * x
    return hx * jnp.tanh(hx) + hx


def _layer_norm(x, g, b):
    mu = jnp.mean(x, axis=-1, keepdims=True)
    xc = x - mu
    var = jnp.mean(xc * xc, axis=-1, keepdims=True)
    return xc * lax.rsqrt(var + EPS) * g + b


def _residual_norm(x, y, gate, g, b):
    return _layer_norm(ALPHA * x + gate * y, g, b)


def _group_mean_matrix(group):
    idx = np.arange(256)
    return jnp.asarray((idx[:, None] // group == idx[None, :] // group) / group, dtype=BF16)


def _group_mean_square(x, gmat_ref):
    sq = (x * x).astype(BF16)
    n = x.shape[1] // 256
    parts = [jnp.dot(sq[:, 256 * j:256 * (j + 1)], gmat_ref[...], preferred_element_type=F32) for j in range(n)]
    return parts[0] if n == 1 else jnp.concatenate(parts, axis=1)


def _seq_pos(i, rows, first_row):
    seq_len = jnp.where(i < P_BLOCKS, SEQ, DEC_SEQ)
    r = lax.broadcasted_iota(jnp.int32, (rows, 1), 0) + (i * TM + first_row)
    return r & (seq_len - 1), seq_len


def _scale_tiles(x, factor, tiles):
    parts, pos = [], 0
    for t in sorted(tiles):
        lo = t * SUBLANES
        if lo > pos:
            parts.append(x[pos:lo])
        parts.append(x[lo:lo + SUBLANES] * factor[lo:lo + SUBLANES])
        pos = lo + SUBLANES
    if pos < x.shape[0]:
        parts.append(x[pos:])
    return jnp.concatenate(parts, axis=0)


MOD_TN = 1536


def _mod_kernel(cond_ref, w_ref, b_ref, o_ref):
    s = _silu(cond_ref[...])
    o_ref[...] = jnp.dot(s, w_ref[...], preferred_element_type=F32) + b_ref[...]


def _modulation(cond, w_mod, b_mod):
    n_out = 6 * D_MODEL
    return pl.pallas_call(
        _mod_kernel,
        grid=(DEPTH, n_out // MOD_TN),
        in_specs=[pl.BlockSpec((MOD_ROWS, D_MODEL), lambda l, j: (0, 0)),
                  pl.BlockSpec((None, D_MODEL, MOD_TN), lambda l, j: (l, 0, j)),
                  pl.BlockSpec((None, 1, MOD_TN), lambda l, j: (l, 0, j))],
        out_specs=pl.BlockSpec((None, MOD_ROWS, MOD_TN), lambda l, j: (l, 0, j)),
        out_shape=jax.ShapeDtypeStruct((DEPTH, MOD_ROWS, n_out), F32),
        compiler_params=_params("arbitrary", "arbitrary"),
        name="modulation",
    )(cond, w_mod, b_mod.reshape(DEPTH, 1, n_out))


def _rope_tables():
    t = np.arange(DEC_SEQ)
    half = ROPE_AXIS_DIM // 2
    inv = (ROPE_THETA ** (-np.arange(0, ROPE_AXIS_DIM, 2, dtype=np.float32) / ROPE_AXIS_DIM)).astype(np.float32)
    row = (t // GRID_W).astype(np.float32)[:, None] * inv
    col = (t % GRID_W).astype(np.float32)[:, None] * inv
    ang = np.concatenate([row, row, col, col], axis=1).astype(np.float32)
    sign = np.concatenate([-np.ones(half), np.ones(half)] * 2).astype(np.float32)
    cos = np.tile(np.cos(ang), (1, 128 // HEAD_DIM))
    sin = np.tile(np.sin(ang) * sign, (1, 128 // HEAD_DIM))
    return jnp.asarray(cos, F32), jnp.asarray(sin, F32)


def _rope(x, cos, sin):
    n = x.shape[1] // 128
    if n > 1:
        cos = jnp.concatenate([cos] * n, axis=1)
        sin = jnp.concatenate([sin] * n, axis=1)
    lane = lax.broadcasted_iota(jnp.int32, x.shape, 1)
    half = ROPE_AXIS_DIM // 2
    partner = jnp.where((lane & (ROPE_AXIS_DIM - 1)) < half,
                        pltpu.roll(x, x.shape[1] - half, 1), pltpu.roll(x, half, 1))
    return x * cos + partner * sin


def _attn_in_kernel(xp_ref, xs_ref, mod_ref, w_ref, qg_ref, kg_ref, gmat_ref, cos_ref, sin_ref,
                    q_ref, k_ref, v_ref, bg_ref, u_ref):
    i = pl.program_id(0)
    m = mod_ref[0]
    x = jnp.where(i < P_BLOCKS, xp_ref[...], xs_ref[...])
    h = (x * (1.0 + m[1:2]) + m[0:1]).astype(BF16)
    z = jnp.dot(h, w_ref[...], preferred_element_type=F32)
    q = z[:, :ATT_WIDTH]
    k = z[:, ATT_WIDTH:ATT_WIDTH + KV_WIDTH]
    o = ATT_WIDTH + 2 * KV_WIDTH
    q = q * lax.rsqrt(_group_mean_square(q, gmat_ref) + EPS) * qg_ref[...]
    kk = (k * k).astype(BF16)
    k_ms = jnp.dot(kk, gmat_ref[:KV_WIDTH, :KV_WIDTH], preferred_element_type=F32)
    k = k * lax.rsqrt(k_ms + EPS) * kg_ref[...]
    v_ref[...] = z[:, ATT_WIDTH + KV_WIDTH:o]
    bg_ref[...] = z[:, o:o + SC_WIDTH].astype(BF16)
    u_ref[...] = z[:, o + SC_WIDTH:o + 2 * SC_WIDTH] * z[:, o + 2 * SC_WIDTH:]

    @pl.when(i < P_BLOCKS)
    def _():
        q_ref[...] = (q * HEAD_DIM ** -0.5).astype(BF16)
        k_ref[...] = k

    @pl.when(i >= P_BLOCKS)
    def _():
        cos, sin = cos_ref[...], sin_ref[...]
        q_ref[...] = (_rope(q, cos, sin) * HEAD_DIM ** -0.5).astype(BF16)
        k_ref[...] = _rope(k, cos, sin)


def _attn_in(x_p, x_s, mod, w_in, q_gain, k_gain):
    cos, sin = _rope_tables()
    row = lambda i: (i, 0)
    rope_row = lambda i: (jnp.maximum(i - P_BLOCKS, 0) % S_BLOCKS_PER_SEQ, 0)
    tok = lambda w, dt: jax.ShapeDtypeStruct((N_TOK, w), dt)
    return pl.pallas_call(
        _attn_in_kernel,
        grid=(N_BLOCKS,),
        in_specs=[*_split_specs(D_MODEL), _mod_spec(),
                  _resident((D_MODEL, EVEN_IN_WIDTH)),
                  _resident((1, ATT_WIDTH)), _resident((1, KV_WIDTH)), _resident((256, 256)),
                  pl.BlockSpec((TM, 128), rope_row), pl.BlockSpec((TM, 128), rope_row)],
        out_specs=[pl.BlockSpec((TM, ATT_WIDTH), row), pl.BlockSpec((TM, KV_WIDTH), row),
                   pl.BlockSpec((TM, KV_WIDTH), row), pl.BlockSpec((TM, SC_WIDTH), row),
                   pl.BlockSpec((TM, SC_WIDTH), row)],
        out_shape=[tok(ATT_WIDTH, BF16), tok(KV_WIDTH, F32), tok(KV_WIDTH, F32), tok(SC_WIDTH, BF16),
                   tok(SC_WIDTH, F32)],
        compiler_params=_params("arbitrary"),
        name="attn_in",
    )(x_p, x_s, mod, w_in.astype(BF16), jnp.tile(q_gain, N_HEADS)[None], jnp.tile(k_gain, N_KV)[None],
      _group_mean_matrix(HEAD_DIM), cos, sin)


def _attend_kernel(q_ref, *refs, n_kv_sets):
    kv_refs, o_ref = refs[:2 * n_kv_sets], refs[-1]
    ks = [r[...].astype(BF16) for r in kv_refs[:n_kv_sets]]
    vs = [r[...].astype(BF16) for r in kv_refs[n_kv_sets:]]
    q = q_ref[...]
    for hd in range(N_HEADS):
        g = hd // Q_GROUP
        qh = q[:, hd * HEAD_DIM:(hd + 1) * HEAD_DIM]
        s = [lax.dot_general(qh, kx[:, g * HEAD_DIM:(g + 1) * HEAD_DIM], NT_DIMS, preferred_element_type=F32)
             for kx in ks]
        mx = functools.reduce(jnp.maximum, [jnp.max(sx, axis=-1, keepdims=True) for sx in s])
        p = [jnp.exp(sx - mx) for sx in s]
        den = functools.reduce(jnp.add, [jnp.sum(px, axis=-1, keepdims=True) for px in p])
        acc = functools.reduce(jnp.add, [
            jnp.dot(px.astype(BF16), vx[:, g * HEAD_DIM:(g + 1) * HEAD_DIM], preferred_element_type=F32)
            for px, vx in zip(p, vs)])
        o_ref[:, hd * HEAD_DIM:(hd + 1) * HEAD_DIM] = (acc / den).astype(BF16)


def _attend_prompt(q, k, v):
    blk = lambda w: pl.BlockSpec((SEQ, w), lambda b: (b, 0))
    return pl.pallas_call(
        functools.partial(_attend_kernel, n_kv_sets=1),
        grid=(BATCH,),
        in_specs=[blk(ATT_WIDTH), blk(KV_WIDTH), blk(KV_WIDTH)],
        out_specs=blk(ATT_WIDTH),
        out_shape=jax.ShapeDtypeStruct((P_TOK, ATT_WIDTH), BF16),
        compiler_params=_params("arbitrary"),
        name="attend_prompt",
    )(q, k, v)


def _attend_latent(q, k, v, ctx_k, ctx_v):
    nq = DEC_SEQ // TQ
    q_off, kv_off = P_TOK // TQ, P_TOK // DEC_SEQ
    lat = pl.BlockSpec((DEC_SEQ, KV_WIDTH), lambda b, j: (kv_off + b, 0))
    ctx = pl.BlockSpec((None, PAST_LEN, KV_WIDTH), lambda b, j: (b, 0, 0))
    return pl.pallas_call(
        functools.partial(_attend_kernel, n_kv_sets=2),
        grid=(DEC_BATCH, nq),
        in_specs=[pl.BlockSpec((TQ, ATT_WIDTH), lambda b, j: (q_off + b * nq + j, 0)), lat, ctx, lat, ctx],
        out_specs=pl.BlockSpec((TQ, ATT_WIDTH), lambda b, j: (b * nq + j, 0)),
        out_shape=jax.ShapeDtypeStruct((S_TOK, ATT_WIDTH), BF16),
        compiler_params=_params("arbitrary", "arbitrary"),
        name="attend_latent",
    )(q, k, ctx_k, v, ctx_v)


def _halo_specs(width):
    per = TM // HALO
    prev = pl.BlockSpec((HALO, width), lambda i: (jnp.maximum(i * per - 1, 0), 0))
    nxt = pl.BlockSpec((HALO, width), lambda i: (jnp.minimum((i + 1) * per, N_TOK // HALO - 1), 0))
    return prev, nxt


def _attn_out_kernel(xp_ref, xs_ref, mod_ref, attp_ref, atts_ref, bg_ref, u_ref, up_ref, un_ref, cw_ref, w_ref,
                     g_ref, b_ref, o_ref):
    i = pl.program_id(0)
    m = mod_ref[0]
    pos, seq_len = _seq_pos(i, TM, 0)
    r = lax.broadcasted_iota(jnp.int32, (TM, 1), 0)
    u = u_ref[...]
    u_prev = jnp.where(r == 0, up_ref[HALO - 1:HALO, :], pltpu.roll(u, 1, 0))
    u_next = jnp.where(r == TM - 1, un_ref[0:1, :], pltpu.roll(u, TM - 1, 0))
    u_prev = jnp.where(pos == 0, 0.0, u_prev)
    u_next = jnp.where(pos == seq_len - 1, 0.0, u_next)
    cw = cw_ref[...]
    sc = bg_ref[...].astype(F32) * (cw[0:1] * u_prev + cw[1:2] * u + cw[2:3] * u_next)
    att = jnp.where(i < P_BLOCKS, attp_ref[...], atts_ref[...])
    y = (jnp.dot(att, w_ref[:ATT_WIDTH, :], preferred_element_type=F32)
         + jnp.dot(sc.astype(BF16), w_ref[ATT_WIDTH:, :], preferred_element_type=F32))
    x = jnp.where(i < P_BLOCKS, xp_ref[...], xs_ref[...])
    o_ref[...] = _residual_norm(x, y, m[2:3], g_ref[...], b_ref[...])


def _attn_out(x_p, x_s, mod, att_p, att_s, bg, u, conv_w, w_out, g, b):
    row = lambda i: (i, 0)
    up, un = _halo_specs(SC_WIDTH)
    return pl.pallas_call(
        _attn_out_kernel,
        grid=(N_BLOCKS,),
        in_specs=[*_split_specs(D_MODEL), _mod_spec(),
                  *_split_specs(ATT_WIDTH), pl.BlockSpec((TM, SC_WIDTH), row),
                  pl.BlockSpec((TM, SC_WIDTH), row), up, un,
                  _resident((CONV_WIDTH, SC_WIDTH)), _resident((D_MODEL, D_MODEL)),
                  _resident((1, D_MODEL)), _resident((1, D_MODEL))],
        out_specs=pl.BlockSpec((TM, D_MODEL), row),
        out_shape=jax.ShapeDtypeStruct((N_TOK, D_MODEL), F32),
        compiler_params=_params("arbitrary"),
        name="attn_out",
    )(x_p, x_s, mod, att_p, att_s, bg, u, u, u, conv_w, w_out.astype(BF16), g[None], b[None])


_FFN_FIRST_TILES = tuple((HALO + n * SEQ) // SUBLANES for n in range(TM // SEQ))
_FFN_LAST_TILES = tuple((HALO + n * SEQ - 1) // SUBLANES for n in range(1, TM // SEQ + 1))


def _ffn_kernel(x_ref, xp_ref, xn_ref, mod_ref, wu_ref, cw_ref, wd_ref, g_ref, b_ref, *refs, split_out):
    o_refs, (h_ref, act_ref), u_refs = refs[:-6], refs[-6:-4], refs[-4:]
    u_ref = lambda slot, part: u_refs[2 * slot + part]
    i = pl.program_id(0)
    m = mod_ref[0]
    rows = TM + 2 * HALO
    scale, shift = 1.0 + m[4:5], m[3:4]
    h_ref[:HALO, :] = (xp_ref[...] * scale + shift).astype(BF16)
    h_ref[HALO:HALO + TM, :] = (x_ref[...] * scale + shift).astype(BF16)
    h_ref[HALO + TM:, :] = (xn_ref[...] * scale + shift).astype(BF16)
    pos, seq_len = _seq_pos(i, rows, -HALO)
    keep_prev = (pos != 0).astype(F32)
    keep_next = (pos != seq_len - 1).astype(F32)
    n_chunks = D_FF // FF_CHUNK

    def cols(c):
        return c * FF_CHUNK, D_FF + c * FF_CHUNK

    def up(c, slot):
        h = h_ref[...]
        for part, col in enumerate(cols(c)):
            u_ref(slot, part)[...] = jnp.dot(h, wu_ref[:, pl.ds(col, FF_CHUNK)], preferred_element_type=F32)

    def conv(slot, part, col):
        w = cw_ref[:, pl.ds(col, FF_CHUNK)]
        u = u_ref(slot, part)[...]
        u_prev = _scale_tiles(pltpu.roll(u, 1, 0), keep_prev, _FFN_FIRST_TILES)
        u_next = _scale_tiles(pltpu.roll(u, rows - 1, 0), keep_next, _FFN_LAST_TILES)
        return (w[0:1] * u_prev + w[1:2] * u + w[2:3] * u_next)[HALO:HALO + TM]

    def gate(c, slot):
        col_a, col_g = cols(c)
        act_ref[:, pl.ds(col_a, FF_CHUNK)] = (_silu(conv(slot, 1, col_g)) * conv(slot, 0, col_a)).astype(BF16)

    up(0, 0)
    for c in range(1, n_chunks):
        gate(c - 1, (c - 1) % 2)
        up(c, c % 2)
    gate(n_chunks - 1, (n_chunks - 1) % 2)
    y = jnp.dot(act_ref[...], wd_ref[...], preferred_element_type=F32)
    y = _residual_norm(x_ref[...], y, m[5:6], g_ref[...], b_ref[...])
    if split_out:
        @pl.when(i < P_BLOCKS)
        def _():
            o_refs[0][...] = y

        @pl.when(i >= P_BLOCKS)
        def _():
            o_refs[1][...] = y
    else:
        o_refs[0][...] = y


def _ffn(x, mod, w_up, conv_w, w_down, g, b, split_out):
    row = lambda i: (i, 0)
    xp, xn = _halo_specs(D_MODEL)
    if split_out:
        out_specs = list(_split_specs(D_MODEL))
        out_shape = [jax.ShapeDtypeStruct((P_TOK, D_MODEL), F32), jax.ShapeDtypeStruct((S_TOK, D_MODEL), F32)]
    else:
        out_specs = pl.BlockSpec((TM, D_MODEL), row)
        out_shape = jax.ShapeDtypeStruct((N_TOK, D_MODEL), F32)
    return pl.pallas_call(
        functools.partial(_ffn_kernel, split_out=split_out),
        grid=(N_BLOCKS,),
        in_specs=[pl.BlockSpec((TM, D_MODEL), row), xp, xn, _mod_spec(),
                  _resident((D_MODEL, 2 * D_FF)), _resident((CONV_WIDTH, 2 * D_FF)), _resident((D_FF, D_MODEL)),
                  _resident((1, D_MODEL)), _resident((1, D_MODEL))],
        out_specs=out_specs,
        out_shape=out_shape,
        scratch_shapes=[pltpu.VMEM((TM + 2 * HALO, D_MODEL), BF16),
                        pltpu.VMEM((TM, D_FF), BF16)] + [pltpu.VMEM((TM + 2 * HALO, FF_CHUNK), F32)] * 4,
        compiler_params=_params("arbitrary"),
        name="conv_ffn",
    )(x, x, x, mod, w_up.astype(BF16), conv_w, w_down.astype(BF16), g[None], b[None])


QIG_WIDTH = 3 * C_WIDTH
FF_WIDTH = 2 * C_WIDTH


def _hgrn_in_kernel(x_ref, mod_ref, w_ref, qig_ref, ff_ref):
    m = mod_ref[0]
    h = (x_ref[...] * (1.0 + m[1:2]) + m[0:1]).astype(BF16)
    for j in range(5):
        z = jnp.dot(h, w_ref[:, j * C_WIDTH:(j + 1) * C_WIDTH], preferred_element_type=F32)
        if j < 3:
            qig_ref[:, j * C_WIDTH:(j + 1) * C_WIDTH] = z.astype(BF16)
        else:
            ff_ref[:, (j - 3) * C_WIDTH:(j - 2) * C_WIDTH] = z


def _hgrn_in(x, mod, w_in):
    row = lambda i: (i, 0)
    return pl.pallas_call(
        _hgrn_in_kernel,
        grid=(N_BLOCKS,),
        in_specs=[pl.BlockSpec((TM, D_MODEL), row), _mod_spec(), _resident((D_MODEL, 5 * C_WIDTH))],
        out_specs=[pl.BlockSpec((TM, QIG_WIDTH), row), pl.BlockSpec((TM, FF_WIDTH), row)],
        out_shape=[jax.ShapeDtypeStruct((N_TOK, QIG_WIDTH), BF16), jax.ShapeDtypeStruct((N_TOK, FF_WIDTH), F32)],
        compiler_params=_params("arbitrary"),
        name="hgrn_in",
    )(x, mod, w_in.astype(BF16))


SCAN_LEVELS = tuple(1 << j for j in range(SCAN_CHUNK.bit_length() - 1))
FINE_LEVELS = tuple(h for h in SCAN_LEVELS if h < SUBLANES)
N_LEVELS = len(SCAN_LEVELS)
DIAG_CODE = 2 * N_LEVELS


def _is_query_side(t, half, forward):
    right = (t & half) != 0
    return right if forward else ~right


def _decay_matrix(forward):
    idx = np.arange(SCAN_CHUNK)
    t, s = idx[:, None], idx[None, :]
    cum = (s <= t) if forward else (s >= t)
    mats = [cum.astype(np.float32)]
    for half in FINE_LEVELS:
        ref = (idx // (2 * half)) * (2 * half) + (half - 1 if forward else half)
        sign = np.where(_is_query_side(idx, half, forward), 1.0, -1.0)[:, None]
        mats.append(sign * (cum.astype(np.float32) - cum[ref].astype(np.float32)))
    mat = np.concatenate(mats, axis=0)
    return jnp.asarray(np.concatenate([mat, mat], axis=1), dtype=BF16)


def _pair_codes():
    idx = np.arange(SCAN_CHUNK)
    t, s = idx[:, None], idx[None, :]
    lev = np.floor(np.log2(np.maximum(t ^ s, 1))).astype(np.int32)
    code = np.where(t > s, lev, np.where(t < s, N_LEVELS + lev, DIAG_CODE))
    return jnp.asarray(code, dtype=jnp.int32)


def _decay_terms(f_raw, lb):
    f = lb + (1.0 - lb) * _sigmoid(f_raw)
    g = jnp.log2(f)
    hi = g.astype(BF16)
    lo = (g - hi.astype(F32)).astype(BF16)
    return jnp.concatenate([hi, lo], axis=0), 1.0 - f


def _fine_levels(q, k, expo, code, forward, scores):
    C = SCAN_CHUNK
    t = lax.broadcasted_iota(jnp.int32, (C, 1), 0)
    base = 0 if forward else N_LEVELS
    for j, half in enumerate(FINE_LEVELS):
        w = jnp.exp2(expo[(j + 1) * C:(j + 2) * C])
        x = (jnp.where(_is_query_side(t, half, forward), q, k) * w).astype(BF16)
        prod = lax.dot_general(x, x, NT_DIMS, preferred_element_type=F32)
        scores = jnp.where(code == base + j, prod, scores)
    return scores


def _coarse_levels(q, k, cum, code, forward, rows):
    C = SCAN_CHUNK
    base = 0 if forward else N_LEVELS
    for j, half in enumerate(SCAN_LEVELS):
        if half in FINE_LEVELS:
            continue
        blk = 2 * half
        xq, xk, q_rows = [], [], []
        zero = jnp.zeros((half, C_DK), F32)
        for n in range(C // blk):
            lo, hi = slice(n * blk, n * blk + half), slice(n * blk + half, (n + 1) * blk)
            if forward:
                ref = cum[n * blk + half - 1:n * blk + half]
                xk += [k[lo] * jnp.exp2(ref - cum[lo]), zero]
                xq.append(q[hi] * jnp.exp2(cum[hi] - ref))
                q_rows.append(n * blk + half)
            else:
                ref = cum[n * blk + half:n * blk + half + 1]
                xq.append(q[lo] * jnp.exp2(cum[lo] - ref))
                xk += [zero, k[hi] * jnp.exp2(ref - cum[hi])]
                q_rows.append(n * blk)
        prod = lax.dot_general(jnp.concatenate(xq, axis=0).astype(BF16), jnp.concatenate(xk, axis=0).astype(BF16),
                               NT_DIMS, preferred_element_type=F32)
        for n, r0 in enumerate(q_rows):
            for i in range(half // SUBLANES):
                rb, src = r0 // SUBLANES + i, n * half + SUBLANES * i
                own = code[SUBLANES * rb:SUBLANES * (rb + 1)] == base + j
                rows[rb] = jnp.where(own, prod[src:src + SUBLANES], rows[rb])
    return rows


def _hgrn_scan_kernel(*refs, sub_len, has_init):
    q_ref, v_ref, ff_ref, fb_ref, lbl_ref, dmf_ref, dmb_ref, code_ref = refs[:8]
    refs = refs[8:]
    if has_init:
        s0_ref, refs = refs[0], refs[1:]
    o_ref, refs = refs[0], refs[1:]
    if not has_init:
        sout_ref, refs = refs[0], refs[1:]
    (sc_ref, qdf_ref, qdb_ref, decf_ref, decb_ref, incf_ref, incb_ref, stf_ref, stb_ref,
     expo_ref, k_ref, kd_ref, g2_ref) = refs
    C = SCAN_CHUNK
    n_chunks = SCAN_BLOCK // C
    sub_chunks = sub_len // C

    lg = lbl_ref[...]
    ex = jnp.exp(lg - jnp.max(lg, axis=0, keepdims=True))
    prob = ex / jnp.sum(ex, axis=0, keepdims=True)
    lb = jnp.sum(prob[1:DEPTH], axis=0)
    lb_f, lb_b = lb[0:1], lb[1:2]
    code = code_ref[...]

    def rows_of(c):
        return pl.ds(c * C if isinstance(c, int) else pl.multiple_of(c * C, C), C)

    def increments(c4, slot):
        v4 = v_ref[rows_of(c4), :]
        incf_ref[c4] = lax.dot_general(v4, kd_ref[1 - slot, 0], TN_DIMS, preferred_element_type=F32)
        incb_ref[c4] = lax.dot_general(v4, kd_ref[1 - slot, 1], TN_DIMS, preferred_element_type=F32)

    def pair_weights(c2, slot):
        r2 = rows_of(c2)
        q = q_ref[r2, :].astype(F32)
        expo_f, expo_b = expo_ref[1 - slot, 0], expo_ref[1 - slot, 1]
        k_f, k_b = k_ref[slot, 0], k_ref[slot, 1]
        cum_f, cum_b = expo_f[:C], expo_b[:C]
        scores = _fine_levels(q, k_f, expo_f, code, True, jnp.zeros((C, C), F32))
        scores = _fine_levels(q, k_b, expo_b, code, False, scores)
        rows = [scores[SUBLANES * n:SUBLANES * (n + 1)] for n in range(C // SUBLANES)]
        rows = _coarse_levels(q, k_f, cum_f, code, True, rows)
        rows = _coarse_levels(q, k_b, cum_b, code, False, rows)
        same_row = jnp.sum(q * (k_f + k_b), axis=-1, keepdims=True)
        sc_ref[c2] = jnp.where(code == DIAG_CODE, same_row, jnp.concatenate(rows, axis=0)).astype(BF16)
        last_f, last_b = cum_f[C - 1:C, :], cum_b[0:1, :]
        qdf_ref[r2, :] = (q * jnp.exp2(cum_f)).astype(BF16)
        qdb_ref[r2, :] = (q * jnp.exp2(cum_b)).astype(BF16)
        decf_ref[pl.ds(c2, 1), :] = jnp.exp2(last_f)
        decb_ref[pl.ds(c2, 1), :] = jnp.exp2(last_b)
        kd_ref[slot, 0] = (k_f * jnp.exp2(last_f - cum_f)).astype(BF16)
        kd_ref[slot, 1] = (k_b * jnp.exp2(last_b - cum_b)).astype(BF16)

    def cumulate(slot):
        expo_ref[slot, 0] = jnp.dot(dmf_ref[...], g2_ref[1 - slot, 0], preferred_element_type=F32)
        expo_ref[slot, 1] = jnp.dot(dmb_ref[...], g2_ref[1 - slot, 1], preferred_element_type=F32)

    def decays(c, slot):
        r1 = rows_of(c)
        g2_ref[slot, 0], k_ref[slot, 0] = _decay_terms(ff_ref[r1, :], lb_f)
        g2_ref[slot, 1], k_ref[slot, 1] = _decay_terms(fb_ref[r1, :], lb_b)

    for t in range(n_chunks + 3):
        slot = t % 2
        if 0 <= t - 3 < n_chunks:
            increments(t - 3, slot)
        if 0 <= t - 2 < n_chunks:
            pair_weights(t - 2, slot)
        if 0 <= t - 1 < n_chunks:
            cumulate(slot)
        if t < n_chunks:
            decays(t, slot)

    for sub in range(SCAN_BLOCK // sub_len):
        first = sub * sub_chunks
        if has_init:
            init = (s0_ref[0].T, s0_ref[1].T)
        else:
            init = (jnp.zeros((C_DV, C_DK), F32),) * 2

        st_f, st_b = init
        for c in range(sub_chunks):
            cf, cb = first + c, first + sub_chunks - 1 - c
            stf_ref[cf] = st_f.astype(BF16)
            stb_ref[cb] = st_b.astype(BF16)
            st_f = st_f * decf_ref[pl.ds(cf, 1), :] + incf_ref[cf]
            st_b = st_b * decb_ref[pl.ds(cb, 1), :] + incb_ref[cb]
        if not has_init:
            sout_ref[sub, 0] = st_f.T
            sout_ref[sub, 1] = st_b.T

    for c in range(n_chunks):
        r = rows_of(c)
        o_ref[r, :] = (jnp.dot(sc_ref[c], v_ref[r, :], preferred_element_type=F32)
                       + lax.dot_general(qdf_ref[r, :], stf_ref[c], NT_DIMS, preferred_element_type=F32)
                       + lax.dot_general(qdb_ref[r, :], stb_ref[c], NT_DIMS, preferred_element_type=F32))


def _hgrn_scan(qig, ff, lb_logits, sub_len, row_off, n_rows, s0):
    n_blocks = n_rows // SCAN_BLOCK
    blk_off = row_off // SCAN_BLOCK
    n_sub = SCAN_BLOCK // sub_len
    heads = C_WIDTH // C_DK
    n_chunks = SCAN_BLOCK // SCAN_CHUNK
    col = lambda part: pl.BlockSpec((SCAN_BLOCK, C_DK), lambda n, h: (blk_off + n, part * heads + h))
    dmat_rows = (1 + len(FINE_LEVELS)) * SCAN_CHUNK
    in_specs = [col(0), col(1), col(0), col(1),
                pl.BlockSpec((DEPTH, 2, C_DK), lambda n, h: (0, 0, h)),
                pl.BlockSpec((dmat_rows, 2 * SCAN_CHUNK), lambda n, h: (0, 0)),
                pl.BlockSpec((dmat_rows, 2 * SCAN_CHUNK), lambda n, h: (0, 0)),
                pl.BlockSpec((SCAN_CHUNK, SCAN_CHUNK), lambda n, h: (0, 0))]
    args = [qig, qig, ff, ff, lb_logits, _decay_matrix(True), _decay_matrix(False), _pair_codes()]
    state_spec = lambda n_seq: pl.BlockSpec((n_seq, 2, None, C_DK, C_DV), lambda n, h: (n, 0, h, 0, 0))
    if s0 is not None:
        assert n_sub == 1
        in_specs.append(pl.BlockSpec((None, 2, None, C_DK, C_DV), lambda n, h: (n, 0, h, 0, 0)))
        args.append(s0)
    out_specs = [pl.BlockSpec((SCAN_BLOCK, C_DV), lambda n, h: (n, h))]
    out_shape = [jax.ShapeDtypeStruct((n_rows, C_WIDTH), F32)]
    if s0 is None:
        out_specs.append(state_spec(n_sub))
        out_shape.append(jax.ShapeDtypeStruct((n_blocks * n_sub, 2, heads, C_DK, C_DV), F32))
    chunk_mats = lambda dt: pltpu.VMEM((n_chunks, SCAN_CHUNK, SCAN_CHUNK), dt)
    return pl.pallas_call(
        functools.partial(_hgrn_scan_kernel, sub_len=sub_len, has_init=s0 is not None),
        grid=(n_blocks, heads),
        in_specs=in_specs, out_specs=out_specs, out_shape=out_shape,
        scratch_shapes=[chunk_mats(BF16), pltpu.VMEM((SCAN_BLOCK, C_DK), BF16), pltpu.VMEM((SCAN_BLOCK, C_DK), BF16),
                        pltpu.VMEM((n_chunks, C_DK), F32), pltpu.VMEM((n_chunks, C_DK), F32),
                        chunk_mats(F32), chunk_mats(F32), chunk_mats(BF16), chunk_mats(BF16),
                        pltpu.VMEM((2, 2, dmat_rows, C_DK), F32), pltpu.VMEM((2, 2, SCAN_CHUNK, C_DK), F32),
                        pltpu.VMEM((2, 2, SCAN_CHUNK, C_DK), BF16), pltpu.VMEM((2, 2, 2 * SCAN_CHUNK, C_DK), BF16)],
        compiler_params=_params("arbitrary", "arbitrary"),
        name="hgrn_scan_init" if s0 is not None else "hgrn_scan_zero",
    )(*args)


def _hgrn_out_kernel(x_ref, mod_ref, op_ref, os_ref, gate_ref, ng_ref, gmat_ref, w_ref, g_ref, b_ref, out_ref):
    m = mod_ref[0]
    o = jnp.where(pl.program_id(0) < P_BLOCKS, op_ref[...], os_ref[...])
    o = o * lax.rsqrt(_group_mean_square(o, gmat_ref) + EPS) * ng_ref[...] * _silu(gate_ref[...].astype(F32))
    y = jnp.dot(o.astype(BF16), w_ref[...], preferred_element_type=F32)
    out_ref[...] = _residual_norm(x_ref[...], y, m[2:3], g_ref[...], b_ref[...])


def _hgrn_out(x, mod, o_p, o_s, qig, norm_g, w_out, g, b):
    row = lambda i: (i, 0)
    return pl.pallas_call(
        _hgrn_out_kernel,
        grid=(N_BLOCKS,),
        in_specs=[pl.BlockSpec((TM, D_MODEL), row), _mod_spec(),
                  *_split_specs(C_WIDTH), pl.BlockSpec((TM, C_WIDTH), lambda i: (i, 2)),
                  _resident((1, C_WIDTH)), _resident((256, 256)), _resident((C_WIDTH, D_MODEL)),
                  _resident((1, D_MODEL)), _resident((1, D_MODEL))],
        out_specs=pl.BlockSpec((TM, D_MODEL), row),
        out_shape=jax.ShapeDtypeStruct((N_TOK, D_MODEL), F32),
        compiler_params=_params("arbitrary"),
        name="hgrn_out",
    )(x, mod, o_p, o_s, qig, jnp.tile(norm_g, C_HEADS)[None], _group_mean_matrix(C_DV), w_out.astype(BF16), g[None],
      b[None])


def kernel(x_prompt, x_sample, cache_k, cache_v, state_hgrn, c, c_ctx, w_mod, b_mod, ln1_g, ln1_b, ln2_g, ln2_b,
           attn_w_in, attn_q_gain, attn_k_gain, sconv_w, attn_w_out, hgrn_w_in, hgrn_lb_logits, hgrn_norm_g,
           hgrn_w_out, ffn_w_up, ffn_conv_w, ffn_w_down):
    x_p = x_prompt.reshape(P_TOK, D_MODEL)
    x_s = x_sample.reshape(S_TOK, D_MODEL)
    cond = jnp.concatenate([c_ctx[None], c, jnp.zeros((MOD_ROWS - 1 - DEC_BATCH, D_MODEL), F32)], axis=0)
    mod = _modulation(cond, w_mod, b_mod).reshape(DEPTH, MOD_ROWS, 6, D_MODEL)

    q, k, v, bg, u = _attn_in(x_p, x_s, mod[0], attn_w_in[0], attn_q_gain[0], attn_k_gain[0])
    att_p = _attend_prompt(q, k, v)
    att_s = _attend_latent(q, k, v, cache_k[:, 0].reshape(DEC_BATCH, PAST_LEN, KV_WIDTH),
                           cache_v[:, 0].reshape(DEC_BATCH, PAST_LEN, KV_WIDTH))
    x = _attn_out(x_p, x_s, mod[0], att_p, att_s, bg, u, sconv_w[0], attn_w_out[0], ln1_g[0], ln1_b[0])
    x = _ffn(x, mod[0], ffn_w_up[0], ffn_conv_w[0], ffn_w_down[0], ln2_g[0], ln2_b[0], split_out=False)
    new_k = k[:P_TOK].reshape(BATCH, 1, SEQ, N_KV, HEAD_DIM)
    new_v = v[:P_TOK].reshape(BATCH, 1, SEQ, N_KV, HEAD_DIM)

    qig, ff = _hgrn_in(x, mod[1], hgrn_w_in[0])
    o_p, s_new = _hgrn_scan(qig, ff, hgrn_lb_logits, SEQ, 0, P_TOK, None)
    o_s, = _hgrn_scan(qig, ff, hgrn_lb_logits, DEC_SEQ, P_TOK, S_TOK, state_hgrn[:, 0])
    x = _hgrn_out(x, mod[1], o_p, o_s, qig, hgrn_norm_g[0], hgrn_w_out[0], ln1_g[1], ln1_b[1])
    y_p, y_s = _ffn(x, mod[1], ffn_w_up[1], ffn_conv_w[1], ffn_w_down[1], ln2_g[1], ln2_b[1], split_out=True)

    return (y_p.reshape(BATCH, SEQ, D_MODEL), y_s.reshape(DEC_BATCH, DEC_SEQ, D_MODEL), new_k, new_v, s_new[:, None])
```

```python
import functools

import jax
import jax.numpy as jnp
import numpy as np
from jax import lax
from jax.experimental import pallas as pl
from jax.experimental.pallas import tpu as pltpu

F32 = jnp.float32
BF16 = jnp.bfloat16

D_MODEL = 1024
BATCH = 32
SEQ = 256
DEPTH = 2
DEC_BATCH = 4
DEC_SEQ = 2048
PAST_LEN = 512
GRID_W = 64
N_HEADS = 8
N_KV = 2
HEAD_DIM = 64
Q_GROUP = N_HEADS // N_KV
ATT_WIDTH = N_HEADS * HEAD_DIM
KV_WIDTH = N_KV * HEAD_DIM
ROPE_AXIS_DIM = HEAD_DIM // 2
ROPE_THETA = 10000.0
SC_WIDTH = D_MODEL - ATT_WIDTH
CONV_WIDTH = 3
C_HEADS = 8
C_DK = D_MODEL // C_HEADS
C_DV = D_MODEL // C_HEADS
C_WIDTH = C_HEADS * C_DK
D_FF = 2816
EVEN_IN_WIDTH = ATT_WIDTH + 2 * KV_WIDTH + 3 * SC_WIDTH
ALPHA = (2 * DEPTH) ** 0.25
EPS = 1e-6

P_TOK = BATCH * SEQ
S_TOK = DEC_BATCH * DEC_SEQ
N_TOK = P_TOK + S_TOK
MOD_ROWS = 8
SUBLANES = 8
TM = 512
HALO = SUBLANES
FF_CHUNK = 256
FFN_OUT_ROWS = 256
SCAN_CHUNK = 128
SCAN_BLOCK = DEC_SEQ
SCAN_SLOTS = 4
TQ = 256
VMEM_LIMIT = 56 * 1024 * 1024

assert P_TOK % TM == 0 and S_TOK % TM == 0 and DEC_SEQ % TM == 0 and TM % SEQ == 0 and DEC_SEQ % SEQ == 0
assert D_FF % FF_CHUNK == 0 and SEQ % SCAN_CHUNK == 0 and DEC_SEQ % SCAN_CHUNK == 0
assert SCAN_BLOCK % SEQ == 0 and P_TOK % SCAN_BLOCK == 0
P_BLOCKS = P_TOK // TM
N_BLOCKS = N_TOK // TM
S_BLOCKS_PER_SEQ = DEC_SEQ // TM

NT_DIMS = (((1,), (1,)), ((), ()))
TN_DIMS = (((0,), (0,)), ((), ()))


def _params(*sem):
    return pltpu.CompilerParams(dimension_semantics=sem, vmem_limit_bytes=VMEM_LIMIT)


def _mod_row(i):
    return jnp.where(i < P_BLOCKS, 0, 1 + (i - P_BLOCKS) // S_BLOCKS_PER_SEQ)


def _mod_spec():
    return pl.BlockSpec((1, 6, D_MODEL), lambda i: (_mod_row(i), 0, 0))


def _resident(shape):
    nd = len(shape)
    return pl.BlockSpec(shape, lambda *_: (0,) * nd, pipeline_mode=pl.Buffered(1))


def _split_specs(width):
    return (pl.BlockSpec((TM, width), lambda i: (jnp.minimum(i, P_BLOCKS - 1), 0)),
            pl.BlockSpec((TM, width), lambda i: (jnp.maximum(i - P_BLOCKS, 0), 0)))


def _sigmoid(x):
    return 0.5 * jnp.tanh(0.5 * x) + 0.5


def _silu(x):
    hx = 0.5 * x
    return hx * jnp.tanh(hx) + hx


def _layer_norm(x, g, b):
    mu = jnp.mean(x, axis=-1, keepdims=True)
    xc = x - mu
    var = jnp.mean(xc * xc, axis=-1, keepdims=True)
    return xc * lax.rsqrt(var + EPS) * g + b


def _residual_norm(x, y, gate, g, b):
    return _layer_norm(ALPHA * x + gate * y, g, b)


def _group_mean_matrix(group):
    idx = np.arange(256)
    return jnp.asarray((idx[:, None] // group == idx[None, :] // group) / group, dtype=BF16)


def _group_mean_square(x, gmat_ref):
    sq = (x * x).astype(BF16)
    n = x.shape[1] // 256
    parts = [jnp.dot(sq[:, 256 * j:256 * (j + 1)], gmat_ref[...], preferred_element_type=F32) for j in range(n)]
    return parts[0] if n == 1 else jnp.concatenate(parts, axis=1)


def _seq_pos(i, rows, first_row):
    seq_len = jnp.where(i < P_BLOCKS, SEQ, DEC_SEQ)
    r = lax.broadcasted_iota(jnp.int32, (rows, 1), 0) + (i * TM + first_row)
    return r & (seq_len - 1), seq_len


def _scale_tiles(x, factor, tiles):
    parts, pos = [], 0
    for t in sorted(tiles):
        lo = t * SUBLANES
        if lo > pos:
            parts.append(x[pos:lo])
        parts.append(x[lo:lo + SUBLANES] * factor[lo:lo + SUBLANES])
        pos = lo + SUBLANES
    if pos < x.shape[0]:
        parts.append(x[pos:])
    return jnp.concatenate(parts, axis=0)


MOD_TN = 1536


def _mod_kernel(cond_ref, w_ref, b_ref, o_ref):
    s = _silu(cond_ref[...])
    o_ref[...] = jnp.dot(s, w_ref[...], preferred_element_type=F32) + b_ref[...]


def _modulation(cond, w_mod, b_mod):
    n_out = 6 * D_MODEL
    return pl.pallas_call(
        _mod_kernel,
        grid=(DEPTH, n_out // MOD_TN),
        in_specs=[pl.BlockSpec((MOD_ROWS, D_MODEL), lambda l, j: (0, 0)),
                  pl.BlockSpec((None, D_MODEL, MOD_TN), lambda l, j: (l, 0, j)),
                  pl.BlockSpec((None, 1, MOD_TN), lambda l, j: (l, 0, j))],
        out_specs=pl.BlockSpec((None, MOD_ROWS, MOD_TN), lambda l, j: (l, 0, j)),
        out_shape=jax.ShapeDtypeStruct((DEPTH, MOD_ROWS, n_out), F32),
        compiler_params=_params("arbitrary", "arbitrary"),
        name="modulation",
    )(cond, w_mod, b_mod.reshape(DEPTH, 1, n_out))


def _rope_tables():
    t = np.arange(DEC_SEQ)
    half = ROPE_AXIS_DIM // 2
    inv = (ROPE_THETA ** (-np.arange(0, ROPE_AXIS_DIM, 2, dtype=np.float32) / ROPE_AXIS_DIM)).astype(np.float32)
    row = (t // GRID_W).astype(np.float32)[:, None] * inv
    col = (t % GRID_W).astype(np.float32)[:, None] * inv
    ang = np.concatenate([row, row, col, col], axis=1).astype(np.float32)
    sign = np.concatenate([-np.ones(half), np.ones(half)] * 2).astype(np.float32)
    cos = np.tile(np.cos(ang), (1, 128 // HEAD_DIM))
    sin = np.tile(np.sin(ang) * sign, (1, 128 // HEAD_DIM))
    return jnp.asarray(cos, F32), jnp.asarray(sin, F32)


def _rope(x, cos, sin):
    n = x.shape[1] // 128
    if n > 1:
        cos = jnp.concatenate([cos] * n, axis=1)
        sin = jnp.concatenate([sin] * n, axis=1)
    lane = lax.broadcasted_iota(jnp.int32, x.shape, 1)
    half = ROPE_AXIS_DIM // 2
    partner = jnp.where((lane & (ROPE_AXIS_DIM - 1)) < half,
                        pltpu.roll(x, x.shape[1] - half, 1), pltpu.roll(x, half, 1))
    return x * cos + partner * sin


def _attn_in_kernel(xp_ref, xs_ref, mod_ref, w_ref, qg_ref, kg_ref, gmat_ref, cos_ref, sin_ref,
                    q_ref, k_ref, v_ref, bg_ref, u_ref):
    i = pl.program_id(0)
    m = mod_ref[0]
    x = jnp.where(i < P_BLOCKS, xp_ref[...], xs_ref[...])
    h = (x * (1.0 + m[1:2]) + m[0:1]).astype(BF16)
    z = jnp.dot(h, w_ref[...], preferred_element_type=F32)
    q = z[:, :ATT_WIDTH]
    k = z[:, ATT_WIDTH:ATT_WIDTH + KV_WIDTH]
    o = ATT_WIDTH + 2 * KV_WIDTH
    q = q * lax.rsqrt(_group_mean_square(q, gmat_ref) + EPS) * qg_ref[...]
    kk = (k * k).astype(BF16)
    k_ms = jnp.dot(kk, gmat_ref[:KV_WIDTH, :KV_WIDTH], preferred_element_type=F32)
    k = k * lax.rsqrt(k_ms + EPS) * kg_ref[...]
    v_ref[...] = z[:, ATT_WIDTH + KV_WIDTH:o]
    bg_ref[...] = z[:, o:o + SC_WIDTH].astype(BF16)
    u_ref[...] = z[:, o + SC_WIDTH:o + 2 * SC_WIDTH] * z[:, o + 2 * SC_WIDTH:]

    @pl.when(i < P_BLOCKS)
    def _():
        q_ref[...] = (q * HEAD_DIM ** -0.5).astype(BF16)
        k_ref[...] = k

    @pl.when(i >= P_BLOCKS)
    def _():
        cos, sin = cos_ref[...], sin_ref[...]
        q_ref[...] = (_rope(q, cos, sin) * HEAD_DIM ** -0.5).astype(BF16)
        k_ref[...] = _rope(k, cos, sin)


def _attn_in(x_p, x_s, mod, w_in, q_gain, k_gain):
    cos, sin = _rope_tables()
    row = lambda i: (i, 0)
    rope_row = lambda i: (jnp.maximum(i - P_BLOCKS, 0) % S_BLOCKS_PER_SEQ, 0)
    tok = lambda w, dt: jax.ShapeDtypeStruct((N_TOK, w), dt)
    return pl.pallas_call(
        _attn_in_kernel,
        grid=(N_BLOCKS,),
        in_specs=[*_split_specs(D_MODEL), _mod_spec(),
                  _resident((D_MODEL, EVEN_IN_WIDTH)),
                  _resident((1, ATT_WIDTH)), _resident((1, KV_WIDTH)), _resident((256, 256)),
                  pl.BlockSpec((TM, 128), rope_row), pl.BlockSpec((TM, 128), rope_row)],
        out_specs=[pl.BlockSpec((TM, ATT_WIDTH), row), pl.BlockSpec((TM, KV_WIDTH), row),
                   pl.BlockSpec((TM, KV_WIDTH), row), pl.BlockSpec((TM, SC_WIDTH), row),
                   pl.BlockSpec((TM, SC_WIDTH), row)],
        out_shape=[tok(ATT_WIDTH, BF16), tok(KV_WIDTH, F32), tok(KV_WIDTH, F32), tok(SC_WIDTH, BF16),
                   tok(SC_WIDTH, F32)],
        compiler_params=_params("arbitrary"),
        name="attn_in",
    )(x_p, x_s, mod, w_in.astype(BF16), jnp.tile(q_gain, N_HEADS)[None], jnp.tile(k_gain, N_KV)[None],
      _group_mean_matrix(HEAD_DIM), cos, sin)


def _attend_kernel(q_ref, *refs, n_kv_sets):
    kv_refs, o_ref = refs[:2 * n_kv_sets], refs[-1]
    ks = [r[...].astype(BF16) for r in kv_refs[:n_kv_sets]]
    vs = [r[...].astype(BF16) for r in kv_refs[n_kv_sets:]]
    q = q_ref[...]
    for hd in range(N_HEADS):
        g = hd // Q_GROUP
        qh = q[:, hd * HEAD_DIM:(hd + 1) * HEAD_DIM]
        s = [lax.dot_general(qh, kx[:, g * HEAD_DIM:(g + 1) * HEAD_DIM], NT_DIMS, preferred_element_type=F32)
             for kx in ks]
        mx = functools.reduce(jnp.maximum, [jnp.max(sx, axis=-1, keepdims=True) for sx in s])
        p = [jnp.exp(sx - mx) for sx in s]
        den = functools.reduce(jnp.add, [jnp.sum(px, axis=-1, keepdims=True) for px in p])
        acc = functools.reduce(jnp.add, [
            jnp.dot(px.astype(BF16), vx[:, g * HEAD_DIM:(g + 1) * HEAD_DIM], preferred_element_type=F32)
            for px, vx in zip(p, vs)])
        o_ref[:, hd * HEAD_DIM:(hd + 1) * HEAD_DIM] = (acc / den).astype(BF16)


def _attend_prompt(q, k, v):
    blk = lambda w: pl.BlockSpec((SEQ, w), lambda b: (b, 0))
    return pl.pallas_call(
        functools.partial(_attend_kernel, n_kv_sets=1),
        grid=(BATCH,),
        in_specs=[blk(ATT_WIDTH), blk(KV_WIDTH), blk(KV_WIDTH)],
        out_specs=blk(ATT_WIDTH),
        out_shape=jax.ShapeDtypeStruct((P_TOK, ATT_WIDTH), BF16),
        compiler_params=_params("arbitrary"),
        name="attend_prompt",
    )(q, k, v)


def _attend_latent(q, k, v, ctx_k, ctx_v):
    nq = DEC_SEQ // TQ
    q_off, kv_off = P_TOK // TQ, P_TOK // DEC_SEQ
    lat = pl.BlockSpec((DEC_SEQ, KV_WIDTH), lambda b, j: (kv_off + b, 0))
    ctx = pl.BlockSpec((None, PAST_LEN, KV_WIDTH), lambda b, j: (b, 0, 0))
    return pl.pallas_call(
        functools.partial(_attend_kernel, n_kv_sets=2),
        grid=(DEC_BATCH, nq),
        in_specs=[pl.BlockSpec((TQ, ATT_WIDTH), lambda b, j: (q_off + b * nq + j, 0)), lat, ctx, lat, ctx],
        out_specs=pl.BlockSpec((TQ, ATT_WIDTH), lambda b, j: (b * nq + j, 0)),
        out_shape=jax.ShapeDtypeStruct((S_TOK, ATT_WIDTH), BF16),
        compiler_params=_params("arbitrary", "arbitrary"),
        name="attend_latent",
    )(q, k, ctx_k, v, ctx_v)


def _halo_specs(width):
    per = TM // HALO
    prev = pl.BlockSpec((HALO, width), lambda i: (jnp.maximum(i * per - 1, 0), 0))
    nxt = pl.BlockSpec((HALO, width), lambda i: (jnp.minimum((i + 1) * per, N_TOK // HALO - 1), 0))
    return prev, nxt


def _attn_out_kernel(xp_ref, xs_ref, mod_ref, attp_ref, atts_ref, bg_ref, u_ref, up_ref, un_ref, cw_ref, w_ref,
                     g_ref, b_ref, o_ref):
    i = pl.program_id(0)
    m = mod_ref[0]
    pos, seq_len = _seq_pos(i, TM, 0)
    r = lax.broadcasted_iota(jnp.int32, (TM, 1), 0)
    u = u_ref[...]
    u_prev = jnp.where(r == 0, up_ref[HALO - 1:HALO, :], pltpu.roll(u, 1, 0))
    u_next = jnp.where(r == TM - 1, un_ref[0:1, :], pltpu.roll(u, TM - 1, 0))
    u_prev = jnp.where(pos == 0, 0.0, u_prev)
    u_next = jnp.where(pos == seq_len - 1, 0.0, u_next)
    cw = cw_ref[...]
    sc = bg_ref[...].astype(F32) * (cw[0:1] * u_prev + cw[1:2] * u + cw[2:3] * u_next)
    att = jnp.where(i < P_BLOCKS, attp_ref[...], atts_ref[...])
    y = (jnp.dot(att, w_ref[:ATT_WIDTH, :], preferred_element_type=F32)
         + jnp.dot(sc.astype(BF16), w_ref[ATT_WIDTH:, :], preferred_element_type=F32))
    x = jnp.where(i < P_BLOCKS, xp_ref[...], xs_ref[...])
    o_ref[...] = _residual_norm(x, y, m[2:3], g_ref[...], b_ref[...])


def _attn_out(x_p, x_s, mod, att_p, att_s, bg, u, conv_w, w_out, g, b):
    row = lambda i: (i, 0)
    up, un = _halo_specs(SC_WIDTH)
    return pl.pallas_call(
        _attn_out_kernel,
        grid=(N_BLOCKS,),
        in_specs=[*_split_specs(D_MODEL), _mod_spec(),
                  *_split_specs(ATT_WIDTH), pl.BlockSpec((TM, SC_WIDTH), row),
                  pl.BlockSpec((TM, SC_WIDTH), row), up, un,
                  _resident((CONV_WIDTH, SC_WIDTH)), _resident((D_MODEL, D_MODEL)),
                  _resident((1, D_MODEL)), _resident((1, D_MODEL))],
        out_specs=pl.BlockSpec((TM, D_MODEL), row),
        out_shape=jax.ShapeDtypeStruct((N_TOK, D_MODEL), F32),
        compiler_params=_params("arbitrary"),
        name="attn_out",
    )(x_p, x_s, mod, att_p, att_s, bg, u, u, u, conv_w, w_out.astype(BF16), g[None], b[None])


_FFN_FIRST_TILES = tuple((HALO + n * SEQ) // SUBLANES for n in range(TM // SEQ))
_FFN_LAST_TILES = tuple((HALO + n * SEQ - 1) // SUBLANES for n in range(1, TM // SEQ + 1))


def _ffn_kernel(x_ref, xp_ref, xn_ref, mod_ref, wu_ref, cw_ref, wd_ref, g_ref, b_ref, *refs, split_out):
    o_refs, (h_ref, act_ref), u_refs = refs[:-6], refs[-6:-4], refs[-4:]
    u_ref = lambda slot, part: u_refs[2 * slot + part]
    i = pl.program_id(0)
    m = mod_ref[0]
    rows = TM + 2 * HALO
    scale, shift = 1.0 + m[4:5], m[3:4]
    h_ref[:HALO, :] = (xp_ref[...] * scale + shift).astype(BF16)
    h_ref[HALO:HALO + TM, :] = (x_ref[...] * scale + shift).astype(BF16)
    h_ref[HALO + TM:, :] = (xn_ref[...] * scale + shift).astype(BF16)
    pos, seq_len = _seq_pos(i, rows, -HALO)
    keep_prev = (pos != 0).astype(F32)
    keep_next = (pos != seq_len - 1).astype(F32)
    n_chunks = D_FF // FF_CHUNK

    def cols(c):
        return c * FF_CHUNK, D_FF + c * FF_CHUNK

    def up(c, slot):
        h = h_ref[...]
        for part, col in enumerate(cols(c)):
            u_ref(slot, part)[...] = jnp.dot(h, wu_ref[:, pl.ds(col, FF_CHUNK)], preferred_element_type=F32)

    def conv(slot, part, col):
        w = cw_ref[:, pl.ds(col, FF_CHUNK)]
        u = u_ref(slot, part)[...]
        u_prev = _scale_tiles(pltpu.roll(u, 1, 0), keep_prev, _FFN_FIRST_TILES)
        u_next = _scale_tiles(pltpu.roll(u, rows - 1, 0), keep_next, _FFN_LAST_TILES)
        return (w[0:1] * u_prev + w[1:2] * u + w[2:3] * u_next)[HALO:HALO + TM]

    def gate(c, slot):
        col_a, col_g = cols(c)
        act_ref[:, pl.ds(col_a, FF_CHUNK)] = (_silu(conv(slot, 1, col_g)) * conv(slot, 0, col_a)).astype(BF16)

    up(0, 0)
    for c in range(1, n_chunks):
        gate(c - 1, (c - 1) % 2)
        up(c, c % 2)
    gate(n_chunks - 1, (n_chunks - 1) % 2)
    ys = []
    for r0 in range(0, TM, FFN_OUT_ROWS):
        rs = slice(r0, r0 + FFN_OUT_ROWS)
        yr = jnp.dot(act_ref[rs, :], wd_ref[...], preferred_element_type=F32)
        ys.append(_residual_norm(x_ref[rs, :], yr, m[5:6], g_ref[...], b_ref[...]))
    y = jnp.concatenate(ys, axis=0)
    if split_out:
        @pl.when(i < P_BLOCKS)
        def _():
            o_refs[0][...] = y

        @pl.when(i >= P_BLOCKS)
        def _():
            o_refs[1][...] = y
    else:
        o_refs[0][...] = y


def _ffn(x, mod, w_up, conv_w, w_down, g, b, split_out):
    row = lambda i: (i, 0)
    xp, xn = _halo_specs(D_MODEL)
    if split_out:
        out_specs = list(_split_specs(D_MODEL))
        out_shape = [jax.ShapeDtypeStruct((P_TOK, D_MODEL), F32), jax.ShapeDtypeStruct((S_TOK, D_MODEL), F32)]
    else:
        out_specs = pl.BlockSpec((TM, D_MODEL), row)
        out_shape = jax.ShapeDtypeStruct((N_TOK, D_MODEL), F32)
    return pl.pallas_call(
        functools.partial(_ffn_kernel, split_out=split_out),
        grid=(N_BLOCKS,),
        in_specs=[pl.BlockSpec((TM, D_MODEL), row), xp, xn, _mod_spec(),
                  _resident((D_MODEL, 2 * D_FF)), _resident((CONV_WIDTH, 2 * D_FF)), _resident((D_FF, D_MODEL)),
                  _resident((1, D_MODEL)), _resident((1, D_MODEL))],
        out_specs=out_specs,
        out_shape=out_shape,
        scratch_shapes=[pltpu.VMEM((TM + 2 * HALO, D_MODEL), BF16),
                        pltpu.VMEM((TM, D_FF), BF16)] + [pltpu.VMEM((TM + 2 * HALO, FF_CHUNK), F32)] * 4,
        compiler_params=_params("arbitrary"),
        name="conv_ffn",
    )(x, x, x, mod, w_up.astype(BF16), conv_w, w_down.astype(BF16), g[None], b[None])


QIG_WIDTH = 3 * C_WIDTH
FF_WIDTH = 2 * C_WIDTH


def _hgrn_in_kernel(x_ref, mod_ref, w_ref, qig_ref, ff_ref):
    m = mod_ref[0]
    h = (x_ref[...] * (1.0 + m[1:2]) + m[0:1]).astype(BF16)
    for j in range(5):
        z = jnp.dot(h, w_ref[:, j * C_WIDTH:(j + 1) * C_WIDTH], preferred_element_type=F32)
        if j < 3:
            qig_ref[:, j * C_WIDTH:(j + 1) * C_WIDTH] = z.astype(BF16)
        else:
            ff_ref[:, (j - 3) * C_WIDTH:(j - 2) * C_WIDTH] = z


def _hgrn_in(x, mod, w_in):
    row = lambda i: (i, 0)
    return pl.pallas_call(
        _hgrn_in_kernel,
        grid=(N_BLOCKS,),
        in_specs=[pl.BlockSpec((TM, D_MODEL), row), _mod_spec(), _resident((D_MODEL, 5 * C_WIDTH))],
        out_specs=[pl.BlockSpec((TM, QIG_WIDTH), row), pl.BlockSpec((TM, FF_WIDTH), row)],
        out_shape=[jax.ShapeDtypeStruct((N_TOK, QIG_WIDTH), BF16), jax.ShapeDtypeStruct((N_TOK, FF_WIDTH), F32)],
        compiler_params=_params("arbitrary"),
        name="hgrn_in",
    )(x, mod, w_in.astype(BF16))


SCAN_LEVELS = tuple(1 << j for j in range(SCAN_CHUNK.bit_length() - 1))
FINE_LEVELS = tuple(h for h in SCAN_LEVELS if h < SUBLANES)
N_LEVELS = len(SCAN_LEVELS)
DIAG_CODE = 2 * N_LEVELS


def _is_query_side(t, half, forward):
    right = (t & half) != 0
    return right if forward else ~right


def _decay_matrix(forward):
    idx = np.arange(SCAN_CHUNK)
    t, s = idx[:, None], idx[None, :]
    cum = (s <= t) if forward else (s >= t)
    mats = [cum.astype(np.float32)]
    for half in FINE_LEVELS:
        ref = (idx // (2 * half)) * (2 * half) + (half - 1 if forward else half)
        sign = np.where(_is_query_side(idx, half, forward), 1.0, -1.0)[:, None]
        mats.append(sign * (cum.astype(np.float32) - cum[ref].astype(np.float32)))
    mat = np.concatenate(mats, axis=0)
    return jnp.asarray(np.concatenate([mat, mat], axis=1), dtype=BF16)


def _pair_codes():
    idx = np.arange(SCAN_CHUNK)
    t, s = idx[:, None], idx[None, :]
    lev = np.floor(np.log2(np.maximum(t ^ s, 1))).astype(np.int32)
    code = np.where(t > s, lev, np.where(t < s, N_LEVELS + lev, DIAG_CODE))
    return jnp.asarray(code, dtype=jnp.int32)


def _decay_terms(f_raw, lb):
    f = lb + (1.0 - lb) * _sigmoid(f_raw)
    g = jnp.log2(f)
    hi = g.astype(BF16)
    lo = (g - hi.astype(F32)).astype(BF16)
    return jnp.concatenate([hi, lo], axis=0), 1.0 - f


def _fine_products(q, k, expo, forward):
    C = SCAN_CHUNK
    t = lax.broadcasted_iota(jnp.int32, (C, 1), 0)
    prods = []
    for j, half in enumerate(FINE_LEVELS):
        w = jnp.exp2(expo[(j + 1) * C:(j + 2) * C])
        x = (jnp.where(_is_query_side(t, half, forward), q, k) * w).astype(BF16)
        prods.append(lax.dot_general(x, x, NT_DIMS, preferred_element_type=F32))
    return prods


def _select_fine(prods, code, forward, scores):
    base = 0 if forward else N_LEVELS
    for j, prod in enumerate(prods):
        scores = jnp.where(code == base + j, prod, scores)
    return scores


def _coarse_products(q, k, cum, forward):
    C = SCAN_CHUNK
    out = []
    for j, half in enumerate(SCAN_LEVELS):
        if half in FINE_LEVELS:
            continue
        blk = 2 * half
        xq, xk, q_rows = [], [], []
        zero = jnp.zeros((half, C_DK), F32)
        for n in range(C // blk):
            lo, hi = slice(n * blk, n * blk + half), slice(n * blk + half, (n + 1) * blk)
            if forward:
                ref = cum[n * blk + half - 1:n * blk + half]
                xk += [k[lo] * jnp.exp2(ref - cum[lo]), zero]
                xq.append(q[hi] * jnp.exp2(cum[hi] - ref))
                q_rows.append(n * blk + half)
            else:
                ref = cum[n * blk + half:n * blk + half + 1]
                xq.append(q[lo] * jnp.exp2(cum[lo] - ref))
                xk += [zero, k[hi] * jnp.exp2(ref - cum[hi])]
                q_rows.append(n * blk)
        prod = lax.dot_general(jnp.concatenate(xq, axis=0).astype(BF16), jnp.concatenate(xk, axis=0).astype(BF16),
                               NT_DIMS, preferred_element_type=F32)
        out.append((j, half, q_rows, prod))
    return out


def _select_coarse(products, code, forward, rows):
    base = 0 if forward else N_LEVELS
    for j, half, q_rows, prod in products:
        for n, r0 in enumerate(q_rows):
            for i in range(half // SUBLANES):
                rb, src = r0 // SUBLANES + i, n * half + SUBLANES * i
                own = code[SUBLANES * rb:SUBLANES * (rb + 1)] == base + j
                rows[rb] = jnp.where(own, prod[src:src + SUBLANES], rows[rb])
    return rows


def _hgrn_scan_kernel(*refs, sub_len, has_init):
    q_ref, v_ref, ff_ref, fb_ref, lbl_ref, dmf_ref, dmb_ref, code_ref = refs[:8]
    refs = refs[8:]
    if has_init:
        s0_ref, refs = refs[0], refs[1:]
    o_ref, refs = refs[0], refs[1:]
    if not has_init:
        sout_ref, refs = refs[0], refs[1:]
    (sc_ref, qdf_ref, qdb_ref, decf_ref, decb_ref, incf_ref, incb_ref, stf_ref, stb_ref,
     expo_ref, k_ref, kd_ref, g2_ref) = refs
    C = SCAN_CHUNK
    n_chunks = SCAN_BLOCK // C
    sub_chunks = sub_len // C

    lg = lbl_ref[...]
    ex = jnp.exp(lg - jnp.max(lg, axis=0, keepdims=True))
    prob = ex / jnp.sum(ex, axis=0, keepdims=True)
    lb = jnp.sum(prob[1:DEPTH], axis=0)
    lb_f, lb_b = lb[0:1], lb[1:2]
    code = code_ref[...]

    def rows_of(c):
        return pl.ds(c * C if isinstance(c, int) else pl.multiple_of(c * C, C), C)

    def increments(c4):
        v4, slot = v_ref[rows_of(c4), :], c4 % SCAN_SLOTS
        incf_ref[c4] = lax.dot_general(v4, kd_ref[slot, 0], TN_DIMS, preferred_element_type=F32)
        incb_ref[c4] = lax.dot_general(v4, kd_ref[slot, 1], TN_DIMS, preferred_element_type=F32)

    def pair_weights(c2):
        r2, slot = rows_of(c2), c2 % SCAN_SLOTS
        q = q_ref[r2, :].astype(F32)
        expo_f, expo_b = expo_ref[slot, 0], expo_ref[slot, 1]
        k_f, k_b = k_ref[slot, 0], k_ref[slot, 1]
        cum_f, cum_b = expo_f[:C], expo_b[:C]
        fine_f, fine_b = _fine_products(q, k_f, expo_f, True), _fine_products(q, k_b, expo_b, False)
        coarse_f, coarse_b = _coarse_products(q, k_f, cum_f, True), _coarse_products(q, k_b, cum_b, False)
        last_f, last_b = cum_f[C - 1:C, :], cum_b[0:1, :]
        qdf_ref[r2, :] = (q * jnp.exp2(cum_f)).astype(BF16)
        qdb_ref[r2, :] = (q * jnp.exp2(cum_b)).astype(BF16)
        decf_ref[pl.ds(c2, 1), :] = jnp.exp2(last_f)
        decb_ref[pl.ds(c2, 1), :] = jnp.exp2(last_b)
        kd_ref[slot, 0] = (k_f * jnp.exp2(last_f - cum_f)).astype(BF16)
        kd_ref[slot, 1] = (k_b * jnp.exp2(last_b - cum_b)).astype(BF16)
        same_row = jnp.sum(q * (k_f + k_b), axis=-1, keepdims=True)
        scores = _select_fine(fine_f, code, True, jnp.zeros((C, C), F32))
        scores = _select_fine(fine_b, code, False, scores)
        rows = [scores[SUBLANES * n:SUBLANES * (n + 1)] for n in range(C // SUBLANES)]
        rows = _select_coarse(coarse_f, code, True, rows)
        rows = _select_coarse(coarse_b, code, False, rows)
        sc_ref[c2] = jnp.where(code == DIAG_CODE, same_row, jnp.concatenate(rows, axis=0)).astype(BF16)

    def cumulate(c):
        slot = c % SCAN_SLOTS
        expo_ref[slot, 0] = jnp.dot(dmf_ref[...], g2_ref[slot, 0], preferred_element_type=F32)
        expo_ref[slot, 1] = jnp.dot(dmb_ref[...], g2_ref[slot, 1], preferred_element_type=F32)

    def decays(c):
        r1, slot = rows_of(c), c % SCAN_SLOTS
        g2_ref[slot, 0], k_ref[slot, 0] = _decay_terms(ff_ref[r1, :], lb_f)
        g2_ref[slot, 1], k_ref[slot, 1] = _decay_terms(fb_ref[r1, :], lb_b)

    for t in range(n_chunks + 3):
        if 0 <= t - 1 < n_chunks:
            cumulate(t - 1)
        if 0 <= t - 2 < n_chunks:
            pair_weights(t - 2)
        if 0 <= t - 3 < n_chunks:
            increments(t - 3)
        if t < n_chunks:
            decays(t)

    for sub in range(SCAN_BLOCK // sub_len):
        first = sub * sub_chunks
        if has_init:
            init = (s0_ref[0].T, s0_ref[1].T)
        else:
            init = (jnp.zeros((C_DV, C_DK), F32),) * 2

        st_f, st_b = init
        for c in range(sub_chunks):
            cf, cb = first + c, first + sub_chunks - 1 - c
            stf_ref[cf] = st_f.astype(BF16)
            stb_ref[cb] = st_b.astype(BF16)
            st_f = st_f * decf_ref[pl.ds(cf, 1), :] + incf_ref[cf]
            st_b = st_b * decb_ref[pl.ds(cb, 1), :] + incb_ref[cb]
        if not has_init:
            sout_ref[sub, 0] = st_f.T
            sout_ref[sub, 1] = st_b.T

    for c in range(n_chunks):
        r = rows_of(c)
        o_ref[r, :] = (jnp.dot(sc_ref[c], v_ref[r, :], preferred_element_type=F32)
                       + lax.dot_general(qdf_ref[r, :], stf_ref[c], NT_DIMS, preferred_element_type=F32)
                       + lax.dot_general(qdb_ref[r, :], stb_ref[c], NT_DIMS, preferred_element_type=F32))


def _hgrn_scan(qig, ff, lb_logits, sub_len, row_off, n_rows, s0):
    n_blocks = n_rows // SCAN_BLOCK
    blk_off = row_off // SCAN_BLOCK
    n_sub = SCAN_BLOCK // sub_len
    heads = C_WIDTH // C_DK
    n_chunks = SCAN_BLOCK // SCAN_CHUNK
    col = lambda part: pl.BlockSpec((SCAN_BLOCK, C_DK), lambda n, h: (blk_off + n, part * heads + h))
    dmat_rows = (1 + len(FINE_LEVELS)) * SCAN_CHUNK
    in_specs = [col(0), col(1), col(0), col(1),
                pl.BlockSpec((DEPTH, 2, C_DK), lambda n, h: (0, 0, h)),
                pl.BlockSpec((dmat_rows, 2 * SCAN_CHUNK), lambda n, h: (0, 0)),
                pl.BlockSpec((dmat_rows, 2 * SCAN_CHUNK), lambda n, h: (0, 0)),
                pl.BlockSpec((SCAN_CHUNK, SCAN_CHUNK), lambda n, h: (0, 0))]
    args = [qig, qig, ff, ff, lb_logits, _decay_matrix(True), _decay_matrix(False), _pair_codes()]
    state_spec = lambda n_seq: pl.BlockSpec((n_seq, 2, None, C_DK, C_DV), lambda n, h: (n, 0, h, 0, 0))
    if s0 is not None:
        assert n_sub == 1
        in_specs.append(pl.BlockSpec((None, 2, None, C_DK, C_DV), lambda n, h: (n, 0, h, 0, 0)))
        args.append(s0)
    out_specs = [pl.BlockSpec((SCAN_BLOCK, C_DV), lambda n, h: (n, h))]
    out_shape = [jax.ShapeDtypeStruct((n_rows, C_WIDTH), F32)]
    if s0 is None:
        out_specs.append(state_spec(n_sub))
        out_shape.append(jax.ShapeDtypeStruct((n_blocks * n_sub, 2, heads, C_DK, C_DV), F32))
    chunk_mats = lambda dt: pltpu.VMEM((n_chunks, SCAN_CHUNK, SCAN_CHUNK), dt)
    return pl.pallas_call(
        functools.partial(_hgrn_scan_kernel, sub_len=sub_len, has_init=s0 is not None),
        grid=(n_blocks, heads),
        in_specs=in_specs, out_specs=out_specs, out_shape=out_shape,
        scratch_shapes=[chunk_mats(BF16), pltpu.VMEM((SCAN_BLOCK, C_DK), BF16), pltpu.VMEM((SCAN_BLOCK, C_DK), BF16),
                        pltpu.VMEM((n_chunks, C_DK), F32), pltpu.VMEM((n_chunks, C_DK), F32),
                        chunk_mats(F32), chunk_mats(F32), chunk_mats(BF16), chunk_mats(BF16),
                        pltpu.VMEM((SCAN_SLOTS, 2, dmat_rows, C_DK), F32),
                        pltpu.VMEM((SCAN_SLOTS, 2, SCAN_CHUNK, C_DK), F32),
                        pltpu.VMEM((SCAN_SLOTS, 2, SCAN_CHUNK, C_DK), BF16),
                        pltpu.VMEM((SCAN_SLOTS, 2, 2 * SCAN_CHUNK, C_DK), BF16)],
        compiler_params=_params("arbitrary", "arbitrary"),
        name="hgrn_scan_init" if s0 is not None else "hgrn_scan_zero",
    )(*args)


def _hgrn_out_kernel(x_ref, mod_ref, op_ref, os_ref, gate_ref, ng_ref, gmat_ref, w_ref, g_ref, b_ref, out_ref):
    m = mod_ref[0]
    o = jnp.where(pl.program_id(0) < P_BLOCKS, op_ref[...], os_ref[...])
    o = o * lax.rsqrt(_group_mean_square(o, gmat_ref) + EPS) * ng_ref[...] * _silu(gate_ref[...].astype(F32))
    y = jnp.dot(o.astype(BF16), w_ref[...], preferred_element_type=F32)
    out_ref[...] = _residual_norm(x_ref[...], y, m[2:3], g_ref[...], b_ref[...])


def _hgrn_out(x, mod, o_p, o_s, qig, norm_g, w_out, g, b):
    row = lambda i: (i, 0)
    return pl.pallas_call(
        _hgrn_out_kernel,
        grid=(N_BLOCKS,),
        in_specs=[pl.BlockSpec((TM, D_MODEL), row), _mod_spec(),
                  *_split_specs(C_WIDTH), pl.BlockSpec((TM, C_WIDTH), lambda i: (i, 2)),
                  _resident((1, C_WIDTH)), _resident((256, 256)), _resident((C_WIDTH, D_MODEL)),
                  _resident((1, D_MODEL)), _resident((1, D_MODEL))],
        out_specs=pl.BlockSpec((TM, D_MODEL), row),
        out_shape=jax.ShapeDtypeStruct((N_TOK, D_MODEL), F32),
        compiler_params=_params("arbitrary"),
        name="hgrn_out",
    )(x, mod, o_p, o_s, qig, jnp.tile(norm_g, C_HEADS)[None], _group_mean_matrix(C_DV), w_out.astype(BF16), g[None],
      b[None])


def kernel(x_prompt, x_sample, cache_k, cache_v, state_hgrn, c, c_ctx, w_mod, b_mod, ln1_g, ln1_b, ln2_g, ln2_b,
           attn_w_in, attn_q_gain, attn_k_gain, sconv_w, attn_w_out, hgrn_w_in, hgrn_lb_logits, hgrn_norm_g,
           hgrn_w_out, ffn_w_up, ffn_conv_w, ffn_w_down):
    x_p = x_prompt.reshape(P_TOK, D_MODEL)
    x_s = x_sample.reshape(S_TOK, D_MODEL)
    cond = jnp.concatenate([c_ctx[None], c, jnp.zeros((MOD_ROWS - 1 - DEC_BATCH, D_MODEL), F32)], axis=0)
    mod = _modulation(cond, w_mod, b_mod).reshape(DEPTH, MOD_ROWS, 6, D_MODEL)

    q, k, v, bg, u = _attn_in(x_p, x_s, mod[0], attn_w_in[0], attn_q_gain[0], attn_k_gain[0])
    att_p = _attend_prompt(q, k, v)
    att_s = _attend_latent(q, k, v, cache_k[:, 0].reshape(DEC_BATCH, PAST_LEN, KV_WIDTH),
                           cache_v[:, 0].reshape(DEC_BATCH, PAST_LEN, KV_WIDTH))
    x = _attn_out(x_p, x_s, mod[0], att_p, att_s, bg, u, sconv_w[0], attn_w_out[0], ln1_g[0], ln1_b[0])
    x = _ffn(x, mod[0], ffn_w_up[0], ffn_conv_w[0], ffn_w_down[0], ln2_g[0], ln2_b[0], split_out=False)
    new_k = k[:P_TOK].reshape(BATCH, 1, SEQ, N_KV, HEAD_DIM)
    new_v = v[:P_TOK].reshape(BATCH, 1, SEQ, N_KV, HEAD_DIM)

    qig, ff = _hgrn_in(x, mod[1], hgrn_w_in[0])
    o_p, s_new = _hgrn_scan(qig, ff, hgrn_lb_logits, SEQ, 0, P_TOK, None)
    o_s, = _hgrn_scan(qig, ff, hgrn_lb_logits, DEC_SEQ, P_TOK, S_TOK, state_hgrn[:, 0])
    x = _hgrn_out(x, mod[1], o_p, o_s, qig, hgrn_norm_g[0], hgrn_w_out[0], ln1_g[1], ln1_b[1])
    y_p, y_s = _ffn(x, mod[1], ffn_w_up[1], ffn_conv_w[1], ffn_w_down[1], ln2_g[1], ln2_b[1], split_out=True)

    return (y_p.reshape(BATCH, SEQ, D_MODEL), y_s.reshape(DEC_BATCH, DEC_SEQ, D_MODEL), new_k, new_v, s_new[:, None])
```

```python
import functools

import jax
import jax.numpy as jnp
import numpy as np
from jax import lax
from jax.experimental import pallas as pl
from jax.experimental.pallas import tpu as pltpu

F32 = jnp.float32
BF16 = jnp.bfloat16

D_MODEL = 1024
BATCH = 32
SEQ = 256
DEPTH = 2
DEC_BATCH = 4
DEC_SEQ = 2048
PAST_LEN = 512
GRID_W = 64
N_HEADS = 8
N_KV = 2
HEAD_DIM = 64
Q_GROUP = N_HEADS // N_KV
ATT_WIDTH = N_HEADS * HEAD_DIM
KV_WIDTH = N_KV * HEAD_DIM
ROPE_AXIS_DIM = HEAD_DIM // 2
ROPE_THETA = 10000.0
SC_WIDTH = D_MODEL - ATT_WIDTH
CONV_WIDTH = 3
C_HEADS = 8
C_DK = D_MODEL // C_HEADS
C_DV = D_MODEL // C_HEADS
C_WIDTH = C_HEADS * C_DK
D_FF = 2816
EVEN_IN_WIDTH = ATT_WIDTH + 2 * KV_WIDTH + 3 * SC_WIDTH
ALPHA = (2 * DEPTH) ** 0.25
EPS = 1e-6

P_TOK = BATCH * SEQ
S_TOK = DEC_BATCH * DEC_SEQ
N_TOK = P_TOK + S_TOK
MOD_ROWS = 8
SUBLANES = 8
TM = 512
HALO = SUBLANES
FF_CHUNK = 256
FFN_OUT_ROWS = 256
SCAN_CHUNK = 128
SCAN_BLOCK = DEC_SEQ
SCAN_SLOTS = 4
TQ = 256
VMEM_LIMIT = 56 * 1024 * 1024

assert P_TOK % TM == 0 and S_TOK % TM == 0 and DEC_SEQ % TM == 0 and TM % SEQ == 0 and DEC_SEQ % SEQ == 0
assert D_FF % FF_CHUNK == 0 and SEQ % SCAN_CHUNK == 0 and DEC_SEQ % SCAN_CHUNK == 0
assert SCAN_BLOCK % SEQ == 0 and P_TOK % SCAN_BLOCK == 0
P_BLOCKS = P_TOK // TM
N_BLOCKS = N_TOK // TM
S_BLOCKS_PER_SEQ = DEC_SEQ // TM

NT_DIMS = (((1,), (1,)), ((), ()))
TN_DIMS = (((0,), (0,)), ((), ()))


def _params(*sem):
    return pltpu.CompilerParams(dimension_semantics=sem, vmem_limit_bytes=VMEM_LIMIT)


def _mod_row(i):
    return jnp.where(i < P_BLOCKS, 0, 1 + (i - P_BLOCKS) // S_BLOCKS_PER_SEQ)


def _mod_spec():
    return pl.BlockSpec((1, 6, D_MODEL), lambda i: (_mod_row(i), 0, 0))


def _resident(shape):
    nd = len(shape)
    return pl.BlockSpec(shape, lambda *_: (0,) * nd, pipeline_mode=pl.Buffered(1))


def _split_specs(width):
    return (pl.BlockSpec((TM, width), lambda i: (jnp.minimum(i, P_BLOCKS - 1), 0)),
            pl.BlockSpec((TM, width), lambda i: (jnp.maximum(i - P_BLOCKS, 0), 0)))


def _sigmoid(x):
    return 0.5 * jnp.tanh(0.5 * x) + 0.5


def _silu(x):
    hx = 0.5 * x
    return hx * jnp.tanh(hx) + hx


def _layer_norm(x, g, b):
    mu = jnp.mean(x, axis=-1, keepdims=True)
    xc = x - mu
    var = jnp.mean(xc * xc, axis=-1, keepdims=True)
    return xc * lax.rsqrt(var + EPS) * g + b


def _residual_norm(x, y, gate, g, b):
    return _layer_norm(ALPHA * x + gate * y, g, b)


def _group_mean_matrix(group):
    idx = np.arange(256)
    return jnp.asarray((idx[:, None] // group == idx[None, :] // group) / group, dtype=BF16)


def _group_mean_square(x, gmat_ref):
    sq = (x * x).astype(BF16)
    n = x.shape[1] // 256
    parts = [jnp.dot(sq[:, 256 * j:256 * (j + 1)], gmat_ref[...], preferred_element_type=F32) for j in range(n)]
    return parts[0] if n == 1 else jnp.concatenate(parts, axis=1)


def _seq_pos(i, rows, first_row):
    seq_len = jnp.where(i < P_BLOCKS, SEQ, DEC_SEQ)
    r = lax.broadcasted_iota(jnp.int32, (rows, 1), 0) + (i * TM + first_row)
    return r & (seq_len - 1), seq_len


def _scale_tiles(x, factor, tiles):
    parts, pos = [], 0
    for t in sorted(tiles):
        lo = t * SUBLANES
        if lo > pos:
            parts.append(x[pos:lo])
        parts.append(x[lo:lo + SUBLANES] * factor[lo:lo + SUBLANES])
        pos = lo + SUBLANES
    if pos < x.shape[0]:
        parts.append(x[pos:])
    return jnp.concatenate(parts, axis=0)


MOD_TN = 1536


def _mod_kernel(cond_ref, w_ref, b_ref, o_ref):
    s = _silu(cond_ref[...])
    o_ref[...] = jnp.dot(s, w_ref[...], preferred_element_type=F32) + b_ref[...]


def _modulation(cond, w_mod, b_mod):
    n_out = 6 * D_MODEL
    return pl.pallas_call(
        _mod_kernel,
        grid=(DEPTH, n_out // MOD_TN),
        in_specs=[pl.BlockSpec((MOD_ROWS, D_MODEL), lambda l, j: (0, 0)),
                  pl.BlockSpec((None, D_MODEL, MOD_TN), lambda l, j: (l, 0, j)),
                  pl.BlockSpec((None, 1, MOD_TN), lambda l, j: (l, 0, j))],
        out_specs=pl.BlockSpec((None, MOD_ROWS, MOD_TN), lambda l, j: (l, 0, j)),
        out_shape=jax.ShapeDtypeStruct((DEPTH, MOD_ROWS, n_out), F32),
        compiler_params=_params("arbitrary", "arbitrary"),
        name="modulation",
    )(cond, w_mod, b_mod.reshape(DEPTH, 1, n_out))


def _rope_tables():
    t = np.arange(DEC_SEQ)
    half = ROPE_AXIS_DIM // 2
    inv = (ROPE_THETA ** (-np.arange(0, ROPE_AXIS_DIM, 2, dtype=np.float32) / ROPE_AXIS_DIM)).astype(np.float32)
    row = (t // GRID_W).astype(np.float32)[:, None] * inv
    col = (t % GRID_W).astype(np.float32)[:, None] * inv
    ang = np.concatenate([row, row, col, col], axis=1).astype(np.float32)
    sign = np.concatenate([-np.ones(half), np.ones(half)] * 2).astype(np.float32)
    cos = np.tile(np.cos(ang), (1, 128 // HEAD_DIM))
    sin = np.tile(np.sin(ang) * sign, (1, 128 // HEAD_DIM))
    return jnp.asarray(cos, F32), jnp.asarray(sin, F32)


def _rope(x, cos, sin):
    n = x.shape[1] // 128
    if n > 1:
        cos = jnp.concatenate([cos] * n, axis=1)
        sin = jnp.concatenate([sin] * n, axis=1)
    lane = lax.broadcasted_iota(jnp.int32, x.shape, 1)
    half = ROPE_AXIS_DIM // 2
    partner = jnp.where((lane & (ROPE_AXIS_DIM - 1)) < half,
                        pltpu.roll(x, x.shape[1] - half, 1), pltpu.roll(x, half, 1))
    return x * cos + partner * sin


def _attn_in_kernel(xp_ref, xs_ref, mod_ref, w_ref, qg_ref, kg_ref, gmat_ref, cos_ref, sin_ref,
                    q_ref, k_ref, v_ref, bg_ref, u_ref):
    i = pl.program_id(0)
    m = mod_ref[0]
    x = jnp.where(i < P_BLOCKS, xp_ref[...], xs_ref[...])
    h = (x * (1.0 + m[1:2]) + m[0:1]).astype(BF16)
    z = jnp.dot(h, w_ref[...], preferred_element_type=F32)
    q = z[:, :ATT_WIDTH]
    k = z[:, ATT_WIDTH:ATT_WIDTH + KV_WIDTH]
    o = ATT_WIDTH + 2 * KV_WIDTH
    q = q * lax.rsqrt(_group_mean_square(q, gmat_ref) + EPS) * qg_ref[...]
    kk = (k * k).astype(BF16)
    k_ms = jnp.dot(kk, gmat_ref[:KV_WIDTH, :KV_WIDTH], preferred_element_type=F32)
    k = k * lax.rsqrt(k_ms + EPS) * kg_ref[...]
    v_ref[...] = z[:, ATT_WIDTH + KV_WIDTH:o]
    bg_ref[...] = z[:, o:o + SC_WIDTH].astype(BF16)
    u_ref[...] = z[:, o + SC_WIDTH:o + 2 * SC_WIDTH] * z[:, o + 2 * SC_WIDTH:]

    @pl.when(i < P_BLOCKS)
    def _():
        q_ref[...] = (q * HEAD_DIM ** -0.5).astype(BF16)
        k_ref[...] = k

    @pl.when(i >= P_BLOCKS)
    def _():
        cos, sin = cos_ref[...], sin_ref[...]
        q_ref[...] = (_rope(q, cos, sin) * HEAD_DIM ** -0.5).astype(BF16)
        k_ref[...] = _rope(k, cos, sin)


def _attn_in(x_p, x_s, mod, w_in, q_gain, k_gain):
    cos, sin = _rope_tables()
    row = lambda i: (i, 0)
    rope_row = lambda i: (jnp.maximum(i - P_BLOCKS, 0) % S_BLOCKS_PER_SEQ, 0)
    tok = lambda w, dt: jax.ShapeDtypeStruct((N_TOK, w), dt)
    return pl.pallas_call(
        _attn_in_kernel,
        grid=(N_BLOCKS,),
        in_specs=[*_split_specs(D_MODEL), _mod_spec(),
                  _resident((D_MODEL, EVEN_IN_WIDTH)),
                  _resident((1, ATT_WIDTH)), _resident((1, KV_WIDTH)), _resident((256, 256)),
                  pl.BlockSpec((TM, 128), rope_row), pl.BlockSpec((TM, 128), rope_row)],
        out_specs=[pl.BlockSpec((TM, ATT_WIDTH), row), pl.BlockSpec((TM, KV_WIDTH), row),
                   pl.BlockSpec((TM, KV_WIDTH), row), pl.BlockSpec((TM, SC_WIDTH), row),
                   pl.BlockSpec((TM, SC_WIDTH), row)],
        out_shape=[tok(ATT_WIDTH, BF16), tok(KV_WIDTH, F32), tok(KV_WIDTH, F32), tok(SC_WIDTH, BF16),
                   tok(SC_WIDTH, F32)],
        compiler_params=_params("arbitrary"),
        name="attn_in",
    )(x_p, x_s, mod, w_in.astype(BF16), jnp.tile(q_gain, N_HEADS)[None], jnp.tile(k_gain, N_KV)[None],
      _group_mean_matrix(HEAD_DIM), cos, sin)


def _attend_kernel(q_ref, *refs, n_kv_sets):
    kv_refs, o_ref = refs[:2 * n_kv_sets], refs[-1]
    ks = [r[...].astype(BF16) for r in kv_refs[:n_kv_sets]]
    vs = [r[...].astype(BF16) for r in kv_refs[n_kv_sets:]]
    q = q_ref[...]
    for hd in range(N_HEADS):
        g = hd // Q_GROUP
        qh = q[:, hd * HEAD_DIM:(hd + 1) * HEAD_DIM]
        s = [lax.dot_general(qh, kx[:, g * HEAD_DIM:(g + 1) * HEAD_DIM], NT_DIMS, preferred_element_type=F32)
             for kx in ks]
        mx = functools.reduce(jnp.maximum, [jnp.max(sx, axis=-1, keepdims=True) for sx in s])
        p = [jnp.exp(sx - mx) for sx in s]
        den = functools.reduce(jnp.add, [jnp.sum(px, axis=-1, keepdims=True) for px in p])
        acc = functools.reduce(jnp.add, [
            jnp.dot(px.astype(BF16), vx[:, g * HEAD_DIM:(g + 1) * HEAD_DIM], preferred_element_type=F32)
            for px, vx in zip(p, vs)])
        o_ref[:, hd * HEAD_DIM:(hd + 1) * HEAD_DIM] = (acc / den).astype(BF16)


def _attend_prompt(q, k, v):
    blk = lambda w: pl.BlockSpec((SEQ, w), lambda b: (b, 0))
    return pl.pallas_call(
        functools.partial(_attend_kernel, n_kv_sets=1),
        grid=(BATCH,),
        in_specs=[blk(ATT_WIDTH), blk(KV_WIDTH), blk(KV_WIDTH)],
        out_specs=blk(ATT_WIDTH),
        out_shape=jax.ShapeDtypeStruct((P_TOK, ATT_WIDTH), BF16),
        compiler_params=_params("arbitrary"),
        name="attend_prompt",
    )(q, k, v)


def _attend_latent(q, k, v, ctx_k, ctx_v):
    nq = DEC_SEQ // TQ
    q_off, kv_off = P_TOK // TQ, P_TOK // DEC_SEQ
    lat = pl.BlockSpec((DEC_SEQ, KV_WIDTH), lambda b, j: (kv_off + b, 0))
    ctx = pl.BlockSpec((None, PAST_LEN, KV_WIDTH), lambda b, j: (b, 0, 0))
    return pl.pallas_call(
        functools.partial(_attend_kernel, n_kv_sets=2),
        grid=(DEC_BATCH, nq),
        in_specs=[pl.BlockSpec((TQ, ATT_WIDTH), lambda b, j: (q_off + b * nq + j, 0)), lat, ctx, lat, ctx],
        out_specs=pl.BlockSpec((TQ, ATT_WIDTH), lambda b, j: (b * nq + j, 0)),
        out_shape=jax.ShapeDtypeStruct((S_TOK, ATT_WIDTH), BF16),
        compiler_params=_params("arbitrary", "arbitrary"),
        name="attend_latent",
    )(q, k, ctx_k, v, ctx_v)


def _halo_specs(width):
    per = TM // HALO
    prev = pl.BlockSpec((HALO, width), lambda i: (jnp.maximum(i * per - 1, 0), 0))
    nxt = pl.BlockSpec((HALO, width), lambda i: (jnp.minimum((i + 1) * per, N_TOK // HALO - 1), 0))
    return prev, nxt


def _attn_out_kernel(xp_ref, xs_ref, mod_ref, attp_ref, atts_ref, bg_ref, u_ref, up_ref, un_ref, cw_ref, w_ref,
                     g_ref, b_ref, o_ref):
    i = pl.program_id(0)
    m = mod_ref[0]
    pos, seq_len = _seq_pos(i, TM, 0)
    r = lax.broadcasted_iota(jnp.int32, (TM, 1), 0)
    u = u_ref[...]
    u_prev = jnp.where(r == 0, up_ref[HALO - 1:HALO, :], pltpu.roll(u, 1, 0))
    u_next = jnp.where(r == TM - 1, un_ref[0:1, :], pltpu.roll(u, TM - 1, 0))
    u_prev = jnp.where(pos == 0, 0.0, u_prev)
    u_next = jnp.where(pos == seq_len - 1, 0.0, u_next)
    cw = cw_ref[...]
    sc = bg_ref[...].astype(F32) * (cw[0:1] * u_prev + cw[1:2] * u + cw[2:3] * u_next)
    att = jnp.where(i < P_BLOCKS, attp_ref[...], atts_ref[...])
    y = (jnp.dot(att, w_ref[:ATT_WIDTH, :], preferred_element_type=F32)
         + jnp.dot(sc.astype(BF16), w_ref[ATT_WIDTH:, :], preferred_element_type=F32))
    x = jnp.where(i < P_BLOCKS, xp_ref[...], xs_ref[...])
    o_ref[...] = _residual_norm(x, y, m[2:3], g_ref[...], b_ref[...])


def _attn_out(x_p, x_s, mod, att_p, att_s, bg, u, conv_w, w_out, g, b):
    row = lambda i: (i, 0)
    up, un = _halo_specs(SC_WIDTH)
    return pl.pallas_call(
        _attn_out_kernel,
        grid=(N_BLOCKS,),
        in_specs=[*_split_specs(D_MODEL), _mod_spec(),
                  *_split_specs(ATT_WIDTH), pl.BlockSpec((TM, SC_WIDTH), row),
                  pl.BlockSpec((TM, SC_WIDTH), row), up, un,
                  _resident((CONV_WIDTH, SC_WIDTH)), _resident((D_MODEL, D_MODEL)),
                  _resident((1, D_MODEL)), _resident((1, D_MODEL))],
        out_specs=pl.BlockSpec((TM, D_MODEL), row),
        out_shape=jax.ShapeDtypeStruct((N_TOK, D_MODEL), F32),
        compiler_params=_params("arbitrary"),
        name="attn_out",
    )(x_p, x_s, mod, att_p, att_s, bg, u, u, u, conv_w, w_out.astype(BF16), g[None], b[None])


_FFN_FIRST_TILES = tuple((HALO + n * SEQ) // SUBLANES for n in range(TM // SEQ))
_FFN_LAST_TILES = tuple((HALO + n * SEQ - 1) // SUBLANES for n in range(1, TM // SEQ + 1))


def _ffn_kernel(x_ref, xp_ref, xn_ref, mod_ref, wu_ref, cw_ref, wd_ref, g_ref, b_ref, *refs, split_out):
    o_refs, (h_ref, act_ref), u_refs = refs[:-6], refs[-6:-4], refs[-4:]
    u_ref = lambda slot, part: u_refs[2 * slot + part]
    i = pl.program_id(0)
    m = mod_ref[0]
    rows = TM + 2 * HALO
    scale, shift = 1.0 + m[4:5], m[3:4]
    h_ref[:HALO, :] = (xp_ref[...] * scale + shift).astype(BF16)
    h_ref[HALO:HALO + TM, :] = (x_ref[...] * scale + shift).astype(BF16)
    h_ref[HALO + TM:, :] = (xn_ref[...] * scale + shift).astype(BF16)
    pos, seq_len = _seq_pos(i, rows, -HALO)
    keep_prev = (pos != 0).astype(F32)
    keep_next = (pos != seq_len - 1).astype(F32)
    n_chunks = D_FF // FF_CHUNK

    def cols(c):
        return c * FF_CHUNK, D_FF + c * FF_CHUNK

    def up(c, slot):
        h = h_ref[...]
        for part, col in enumerate(cols(c)):
            u_ref(slot, part)[...] = jnp.dot(h, wu_ref[:, pl.ds(col, FF_CHUNK)], preferred_element_type=F32)

    def conv(slot, part, col):
        w = cw_ref[:, pl.ds(col, FF_CHUNK)]
        u = u_ref(slot, part)[...]
        u_prev = _scale_tiles(pltpu.roll(u, 1, 0), keep_prev, _FFN_FIRST_TILES)
        u_next = _scale_tiles(pltpu.roll(u, rows - 1, 0), keep_next, _FFN_LAST_TILES)
        return (w[0:1] * u_prev + w[1:2] * u + w[2:3] * u_next)[HALO:HALO + TM]

    def gate(c, slot):
        col_a, col_g = cols(c)
        act_ref[:, pl.ds(col_a, FF_CHUNK)] = (_silu(conv(slot, 1, col_g)) * conv(slot, 0, col_a)).astype(BF16)

    up(0, 0)
    for c in range(1, n_chunks):
        gate(c - 1, (c - 1) % 2)
        up(c, c % 2)
    gate(n_chunks - 1, (n_chunks - 1) % 2)
    ys = []
    for r0 in range(0, TM, FFN_OUT_ROWS):
        rs = slice(r0, r0 + FFN_OUT_ROWS)
        yr = jnp.dot(act_ref[rs, :], wd_ref[...], preferred_element_type=F32)
        ys.append(_residual_norm(x_ref[rs, :], yr, m[5:6], g_ref[...], b_ref[...]))
    y = jnp.concatenate(ys, axis=0)
    if split_out:
        @pl.when(i < P_BLOCKS)
        def _():
            o_refs[0][...] = y

        @pl.when(i >= P_BLOCKS)
        def _():
            o_refs[1][...] = y
    else:
        o_refs[0][...] = y


def _ffn(x, mod, layer, w_up, conv_w, w_down, g, b, split_out):
    row = lambda i: (i, 0)
    xp, xn = _halo_specs(D_MODEL)
    of_layer = lambda *shape: pl.BlockSpec((None, *shape), lambda i: (layer, 0, 0), pipeline_mode=pl.Buffered(1))
    if split_out:
        out_specs = list(_split_specs(D_MODEL))
        out_shape = [jax.ShapeDtypeStruct((P_TOK, D_MODEL), F32), jax.ShapeDtypeStruct((S_TOK, D_MODEL), F32)]
    else:
        out_specs = pl.BlockSpec((TM, D_MODEL), row)
        out_shape = jax.ShapeDtypeStruct((N_TOK, D_MODEL), F32)
    return pl.pallas_call(
        functools.partial(_ffn_kernel, split_out=split_out),
        grid=(N_BLOCKS,),
        in_specs=[pl.BlockSpec((TM, D_MODEL), row), xp, xn, _mod_spec(),
                  of_layer(D_MODEL, 2 * D_FF), of_layer(CONV_WIDTH, 2 * D_FF), of_layer(D_FF, D_MODEL),
                  _resident((1, D_MODEL)), _resident((1, D_MODEL))],
        out_specs=out_specs,
        out_shape=out_shape,
        scratch_shapes=[pltpu.VMEM((TM + 2 * HALO, D_MODEL), BF16),
                        pltpu.VMEM((TM, D_FF), BF16)] + [pltpu.VMEM((TM + 2 * HALO, FF_CHUNK), F32)] * 4,
        compiler_params=_params("arbitrary"),
        name="conv_ffn",
    )(x, x, x, mod, w_up, conv_w, w_down, g[None], b[None])


QIG_WIDTH = 3 * C_WIDTH
FF_WIDTH = 2 * C_WIDTH


def _hgrn_in_kernel(x_ref, mod_ref, w_ref, qig_ref, ff_ref):
    m = mod_ref[0]
    h = (x_ref[...] * (1.0 + m[1:2]) + m[0:1]).astype(BF16)
    for j in range(5):
        z = jnp.dot(h, w_ref[:, j * C_WIDTH:(j + 1) * C_WIDTH], preferred_element_type=F32)
        if j < 3:
            qig_ref[:, j * C_WIDTH:(j + 1) * C_WIDTH] = z.astype(BF16)
        else:
            ff_ref[:, (j - 3) * C_WIDTH:(j - 2) * C_WIDTH] = z


def _hgrn_in(x, mod, w_in):
    row = lambda i: (i, 0)
    return pl.pallas_call(
        _hgrn_in_kernel,
        grid=(N_BLOCKS,),
        in_specs=[pl.BlockSpec((TM, D_MODEL), row), _mod_spec(), _resident((D_MODEL, 5 * C_WIDTH))],
        out_specs=[pl.BlockSpec((TM, QIG_WIDTH), row), pl.BlockSpec((TM, FF_WIDTH), row)],
        out_shape=[jax.ShapeDtypeStruct((N_TOK, QIG_WIDTH), BF16), jax.ShapeDtypeStruct((N_TOK, FF_WIDTH), F32)],
        compiler_params=_params("arbitrary"),
        name="hgrn_in",
    )(x, mod, w_in.astype(BF16))


SCAN_LEVELS = tuple(1 << j for j in range(SCAN_CHUNK.bit_length() - 1))
FINE_LEVELS = tuple(h for h in SCAN_LEVELS if h < SUBLANES)
N_LEVELS = len(SCAN_LEVELS)
DIAG_CODE = 2 * N_LEVELS


def _is_query_side(t, half, forward):
    right = (t & half) != 0
    return right if forward else ~right


def _decay_matrix(forward):
    idx = np.arange(SCAN_CHUNK)
    t, s = idx[:, None], idx[None, :]
    cum = (s <= t) if forward else (s >= t)
    mats = [cum.astype(np.float32)]
    for half in FINE_LEVELS:
        ref = (idx // (2 * half)) * (2 * half) + (half - 1 if forward else half)
        sign = np.where(_is_query_side(idx, half, forward), 1.0, -1.0)[:, None]
        mats.append(sign * (cum.astype(np.float32) - cum[ref].astype(np.float32)))
    mat = np.concatenate(mats, axis=0)
    return jnp.asarray(np.concatenate([mat, mat], axis=1), dtype=BF16)


def _pair_codes():
    idx = np.arange(SCAN_CHUNK)
    t, s = idx[:, None], idx[None, :]
    lev = np.floor(np.log2(np.maximum(t ^ s, 1))).astype(np.int32)
    code = np.where(t > s, lev, np.where(t < s, N_LEVELS + lev, DIAG_CODE))
    return jnp.asarray(code, dtype=jnp.int32)


def _decay_terms(f_raw, lb):
    f = lb + (1.0 - lb) * _sigmoid(f_raw)
    g = jnp.log2(f)
    hi = g.astype(BF16)
    lo = (g - hi.astype(F32)).astype(BF16)
    return jnp.concatenate([hi, lo], axis=0), 1.0 - f


def _fine_products(q, k, expo, forward):
    C = SCAN_CHUNK
    t = lax.broadcasted_iota(jnp.int32, (C, 1), 0)
    prods = []
    for j, half in enumerate(FINE_LEVELS):
        w = jnp.exp2(expo[(j + 1) * C:(j + 2) * C])
        x = (jnp.where(_is_query_side(t, half, forward), q, k) * w).astype(BF16)
        prods.append(lax.dot_general(x, x, NT_DIMS, preferred_element_type=F32))
    return prods


def _select_fine(prods, code, forward, scores):
    base = 0 if forward else N_LEVELS
    for j, prod in enumerate(prods):
        scores = jnp.where(code == base + j, prod, scores)
    return scores


def _coarse_products(q, k, cum, forward):
    C = SCAN_CHUNK
    out = []
    for j, half in enumerate(SCAN_LEVELS):
        if half in FINE_LEVELS:
            continue
        blk = 2 * half
        xq, xk, q_rows = [], [], []
        zero = jnp.zeros((half, C_DK), F32)
        for n in range(C // blk):
            lo, hi = slice(n * blk, n * blk + half), slice(n * blk + half, (n + 1) * blk)
            if forward:
                ref = cum[n * blk + half - 1:n * blk + half]
                xk += [k[lo] * jnp.exp2(ref - cum[lo]), zero]
                xq.append(q[hi] * jnp.exp2(cum[hi] - ref))
                q_rows.append(n * blk + half)
            else:
                ref = cum[n * blk + half:n * blk + half + 1]
                xq.append(q[lo] * jnp.exp2(cum[lo] - ref))
                xk += [zero, k[hi] * jnp.exp2(ref - cum[hi])]
                q_rows.append(n * blk)
        prod = lax.dot_general(jnp.concatenate(xq, axis=0).astype(BF16), jnp.concatenate(xk, axis=0).astype(BF16),
                               NT_DIMS, preferred_element_type=F32)
        out.append((j, half, q_rows, prod))
    return out


def _select_coarse(products, code, forward, rows):
    base = 0 if forward else N_LEVELS
    for j, half, q_rows, prod in products:
        for n, r0 in enumerate(q_rows):
            for i in range(half // SUBLANES):
                rb, src = r0 // SUBLANES + i, n * half + SUBLANES * i
                own = code[SUBLANES * rb:SUBLANES * (rb + 1)] == base + j
                rows[rb] = jnp.where(own, prod[src:src + SUBLANES], rows[rb])
    return rows


def _hgrn_scan_kernel(*refs, sub_len, has_init):
    q_ref, v_ref, ff_ref, fb_ref, lbl_ref, dmf_ref, dmb_ref, code_ref = refs[:8]
    refs = refs[8:]
    if has_init:
        s0_ref, refs = refs[0], refs[1:]
    o_ref, refs = refs[0], refs[1:]
    if not has_init:
        sout_ref, refs = refs[0], refs[1:]
    (sc_ref, qdf_ref, qdb_ref, decf_ref, decb_ref, incf_ref, incb_ref, stf_ref, stb_ref,
     expo_ref, k_ref, kd_ref, g2_ref) = refs
    C = SCAN_CHUNK
    n_chunks = SCAN_BLOCK // C
    sub_chunks = sub_len // C

    lg = lbl_ref[...]
    ex = jnp.exp(lg - jnp.max(lg, axis=0, keepdims=True))
    prob = ex / jnp.sum(ex, axis=0, keepdims=True)
    lb = jnp.sum(prob[1:DEPTH], axis=0)
    lb_f, lb_b = lb[0:1], lb[1:2]
    code = code_ref[...]

    def rows_of(c):
        return pl.ds(c * C if isinstance(c, int) else pl.multiple_of(c * C, C), C)

    def increments(c4):
        v4, slot = v_ref[rows_of(c4), :], c4 % SCAN_SLOTS
        incf_ref[c4] = lax.dot_general(v4, kd_ref[slot, 0], TN_DIMS, preferred_element_type=F32)
        incb_ref[c4] = lax.dot_general(v4, kd_ref[slot, 1], TN_DIMS, preferred_element_type=F32)

    def pair_weights(c2):
        r2, slot = rows_of(c2), c2 % SCAN_SLOTS
        q = q_ref[r2, :].astype(F32)
        expo_f, expo_b = expo_ref[slot, 0], expo_ref[slot, 1]
        k_f, k_b = k_ref[slot, 0], k_ref[slot, 1]
        cum_f, cum_b = expo_f[:C], expo_b[:C]
        fine_f, fine_b = _fine_products(q, k_f, expo_f, True), _fine_products(q, k_b, expo_b, False)
        coarse_f, coarse_b = _coarse_products(q, k_f, cum_f, True), _coarse_products(q, k_b, cum_b, False)
        last_f, last_b = cum_f[C - 1:C, :], cum_b[0:1, :]
        qdf_ref[r2, :] = (q * jnp.exp2(cum_f)).astype(BF16)
        qdb_ref[r2, :] = (q * jnp.exp2(cum_b)).astype(BF16)
        decf_ref[pl.ds(c2, 1), :] = jnp.exp2(last_f)
        decb_ref[pl.ds(c2, 1), :] = jnp.exp2(last_b)
        kd_ref[slot, 0] = (k_f * jnp.exp2(last_f - cum_f)).astype(BF16)
        kd_ref[slot, 1] = (k_b * jnp.exp2(last_b - cum_b)).astype(BF16)
        same_row = jnp.sum(q * (k_f + k_b), axis=-1, keepdims=True)
        scores = _select_fine(fine_f, code, True, jnp.zeros((C, C), F32))
        scores = _select_fine(fine_b, code, False, scores)
        rows = [scores[SUBLANES * n:SUBLANES * (n + 1)] for n in range(C // SUBLANES)]
        rows = _select_coarse(coarse_f, code, True, rows)
        rows = _select_coarse(coarse_b, code, False, rows)
        sc_ref[c2] = jnp.where(code == DIAG_CODE, same_row, jnp.concatenate(rows, axis=0)).astype(BF16)

    def cumulate(c):
        slot = c % SCAN_SLOTS
        expo_ref[slot, 0] = jnp.dot(dmf_ref[...], g2_ref[slot, 0], preferred_element_type=F32)
        expo_ref[slot, 1] = jnp.dot(dmb_ref[...], g2_ref[slot, 1], preferred_element_type=F32)

    def decays(c):
        r1, slot = rows_of(c), c % SCAN_SLOTS
        g2_ref[slot, 0], k_ref[slot, 0] = _decay_terms(ff_ref[r1, :], lb_f)
        g2_ref[slot, 1], k_ref[slot, 1] = _decay_terms(fb_ref[r1, :], lb_b)

    for t in range(n_chunks + 3):
        if 0 <= t - 1 < n_chunks:
            cumulate(t - 1)
        if 0 <= t - 2 < n_chunks:
            pair_weights(t - 2)
        if t < n_chunks:
            decays(t)
        if 0 <= t - 3 < n_chunks:
            increments(t - 3)

    for sub in range(SCAN_BLOCK // sub_len):
        first = sub * sub_chunks
        if has_init:
            init = (s0_ref[0].T, s0_ref[1].T)
        else:
            init = (jnp.zeros((C_DV, C_DK), F32),) * 2

        st_f, st_b = init
        for c in range(sub_chunks):
            cf, cb = first + c, first + sub_chunks - 1 - c
            stf_ref[cf] = st_f.astype(BF16)
            stb_ref[cb] = st_b.astype(BF16)
            st_f = st_f * decf_ref[pl.ds(cf, 1), :] + incf_ref[cf]
            st_b = st_b * decb_ref[pl.ds(cb, 1), :] + incb_ref[cb]
        if not has_init:
            sout_ref[sub, 0] = st_f.T
            sout_ref[sub, 1] = st_b.T

    for c in range(n_chunks):
        r = rows_of(c)
        o_ref[r, :] = (jnp.dot(sc_ref[c], v_ref[r, :], preferred_element_type=F32)
                       + lax.dot_general(qdf_ref[r, :], stf_ref[c], NT_DIMS, preferred_element_type=F32)
                       + lax.dot_general(qdb_ref[r, :], stb_ref[c], NT_DIMS, preferred_element_type=F32))


def _hgrn_scan(qig, ff, lb_logits, sub_len, row_off, n_rows, s0):
    n_blocks = n_rows // SCAN_BLOCK
    blk_off = row_off // SCAN_BLOCK
    n_sub = SCAN_BLOCK // sub_len
    heads = C_WIDTH // C_DK
    n_chunks = SCAN_BLOCK // SCAN_CHUNK
    col = lambda part: pl.BlockSpec((SCAN_BLOCK, C_DK), lambda n, h: (blk_off + n, part * heads + h))
    dmat_rows = (1 + len(FINE_LEVELS)) * SCAN_CHUNK
    in_specs = [col(0), col(1), col(0), col(1),
                pl.BlockSpec((DEPTH, 2, C_DK), lambda n, h: (0, 0, h)),
                pl.BlockSpec((dmat_rows, 2 * SCAN_CHUNK), lambda n, h: (0, 0)),
                pl.BlockSpec((dmat_rows, 2 * SCAN_CHUNK), lambda n, h: (0, 0)),
                pl.BlockSpec((SCAN_CHUNK, SCAN_CHUNK), lambda n, h: (0, 0))]
    args = [qig, qig, ff, ff, lb_logits, _decay_matrix(True), _decay_matrix(False), _pair_codes()]
    state_spec = lambda n_seq: pl.BlockSpec((n_seq, 2, None, C_DK, C_DV), lambda n, h: (n, 0, h, 0, 0))
    if s0 is not None:
        assert n_sub == 1
        in_specs.append(pl.BlockSpec((None, 2, None, C_DK, C_DV), lambda n, h: (n, 0, h, 0, 0)))
        args.append(s0)
    out_specs = [pl.BlockSpec((SCAN_BLOCK, C_DV), lambda n, h: (n, h))]
    out_shape = [jax.ShapeDtypeStruct((n_rows, C_WIDTH), F32)]
    if s0 is None:
        out_specs.append(state_spec(n_sub))
        out_shape.append(jax.ShapeDtypeStruct((n_blocks * n_sub, 2, heads, C_DK, C_DV), F32))
    chunk_mats = lambda dt: pltpu.VMEM((n_chunks, SCAN_CHUNK, SCAN_CHUNK), dt)
    return pl.pallas_call(
        functools.partial(_hgrn_scan_kernel, sub_len=sub_len, has_init=s0 is not None),
        grid=(n_blocks, heads),
        in_specs=in_specs, out_specs=out_specs, out_shape=out_shape,
        scratch_shapes=[chunk_mats(BF16), pltpu.VMEM((SCAN_BLOCK, C_DK), BF16), pltpu.VMEM((SCAN_BLOCK, C_DK), BF16),
                        pltpu.VMEM((n_chunks, C_DK), F32), pltpu.VMEM((n_chunks, C_DK), F32),
                        chunk_mats(F32), chunk_mats(F32), chunk_mats(BF16), chunk_mats(BF16),
                        pltpu.VMEM((SCAN_SLOTS, 2, dmat_rows, C_DK), F32),
                        pltpu.VMEM((SCAN_SLOTS, 2, SCAN_CHUNK, C_DK), F32),
                        pltpu.VMEM((SCAN_SLOTS, 2, SCAN_CHUNK, C_DK), BF16),
                        pltpu.VMEM((SCAN_SLOTS, 2, 2 * SCAN_CHUNK, C_DK), BF16)],
        compiler_params=_params("arbitrary", "arbitrary"),
        name="hgrn_scan_init" if s0 is not None else "hgrn_scan_zero",
    )(*args)


def _hgrn_out_kernel(x_ref, mod_ref, op_ref, os_ref, gate_ref, ng_ref, gmat_ref, w_ref, g_ref, b_ref, out_ref):
    m = mod_ref[0]
    o = jnp.where(pl.program_id(0) < P_BLOCKS, op_ref[...], os_ref[...])
    o = o * lax.rsqrt(_group_mean_square(o, gmat_ref) + EPS) * ng_ref[...] * _silu(gate_ref[...].astype(F32))
    y = jnp.dot(o.astype(BF16), w_ref[...], preferred_element_type=F32)
    out_ref[...] = _residual_norm(x_ref[...], y, m[2:3], g_ref[...], b_ref[...])


def _hgrn_out(x, mod, o_p, o_s, qig, norm_g, w_out, g, b):
    row = lambda i: (i, 0)
    return pl.pallas_call(
        _hgrn_out_kernel,
        grid=(N_BLOCKS,),
        in_specs=[pl.BlockSpec((TM, D_MODEL), row), _mod_spec(),
                  *_split_specs(C_WIDTH), pl.BlockSpec((TM, C_WIDTH), lambda i: (i, 2)),
                  _resident((1, C_WIDTH)), _resident((256, 256)), _resident((C_WIDTH, D_MODEL)),
                  _resident((1, D_MODEL)), _resident((1, D_MODEL))],
        out_specs=pl.BlockSpec((TM, D_MODEL), row),
        out_shape=jax.ShapeDtypeStruct((N_TOK, D_MODEL), F32),
        compiler_params=_params("arbitrary"),
        name="hgrn_out",
    )(x, mod, o_p, o_s, qig, jnp.tile(norm_g, C_HEADS)[None], _group_mean_matrix(C_DV), w_out.astype(BF16), g[None],
      b[None])


def kernel(x_prompt, x_sample, cache_k, cache_v, state_hgrn, c, c_ctx, w_mod, b_mod, ln1_g, ln1_b, ln2_g, ln2_b,
           attn_w_in, attn_q_gain, attn_k_gain, sconv_w, attn_w_out, hgrn_w_in, hgrn_lb_logits, hgrn_norm_g,
           hgrn_w_out, ffn_w_up, ffn_conv_w, ffn_w_down):
    x_p = x_prompt.reshape(P_TOK, D_MODEL)
    x_s = x_sample.reshape(S_TOK, D_MODEL)
    cond = jnp.concatenate([c_ctx[None], c, jnp.zeros((MOD_ROWS - 1 - DEC_BATCH, D_MODEL), F32)], axis=0)
    mod = _modulation(cond, w_mod, b_mod).reshape(DEPTH, MOD_ROWS, 6, D_MODEL)

    q, k, v, bg, u = _attn_in(x_p, x_s, mod[0], attn_w_in[0], attn_q_gain[0], attn_k_gain[0])
    att_p = _attend_prompt(q, k, v)
    att_s = _attend_latent(q, k, v, cache_k[:, 0].reshape(DEC_BATCH, PAST_LEN, KV_WIDTH),
                           cache_v[:, 0].reshape(DEC_BATCH, PAST_LEN, KV_WIDTH))
    x = _attn_out(x_p, x_s, mod[0], att_p, att_s, bg, u, sconv_w[0], attn_w_out[0], ln1_g[0], ln1_b[0])
    w_up, w_down = ffn_w_up.astype(BF16), ffn_w_down.astype(BF16)
    x = _ffn(x, mod[0], 0, w_up, ffn_conv_w, w_down, ln2_g[0], ln2_b[0], split_out=False)
    new_k = k[:P_TOK].reshape(BATCH, 1, SEQ, N_KV, HEAD_DIM)
    new_v = v[:P_TOK].reshape(BATCH, 1, SEQ, N_KV, HEAD_DIM)

    qig, ff = _hgrn_in(x, mod[1], hgrn_w_in[0])
    o_p, s_new = _hgrn_scan(qig, ff, hgrn_lb_logits, SEQ, 0, P_TOK, None)
    o_s, = _hgrn_scan(qig, ff, hgrn_lb_logits, DEC_SEQ, P_TOK, S_TOK, state_hgrn[:, 0])
    x = _hgrn_out(x, mod[1], o_p, o_s, qig, hgrn_norm_g[0], hgrn_w_out[0], ln1_g[1], ln1_b[1])
    y_p, y_s = _ffn(x, mod[1], 1, w_up, ffn_conv_w, w_down, ln2_g[1], ln2_b[1], split_out=True)

    return (y_p.reshape(BATCH, SEQ, D_MODEL), y_s.reshape(DEC_BATCH, DEC_SEQ, D_MODEL), new_k, new_v, s_new[:, None])
```

```python
import functools

import jax
import jax.numpy as jnp
import numpy as np
from jax import lax
from jax.experimental import pallas as pl
from jax.experimental.pallas import tpu as pltpu

F32 = jnp.float32
BF16 = jnp.bfloat16

D_MODEL = 1024
BATCH = 32
SEQ = 256
DEPTH = 2
DEC_BATCH = 4
DEC_SEQ = 2048
PAST_LEN = 512
GRID_W = 64
N_HEADS = 8
N_KV = 2
HEAD_DIM = 64
Q_GROUP = N_HEADS // N_KV
ATT_WIDTH = N_HEADS * HEAD_DIM
KV_WIDTH = N_KV * HEAD_DIM
ROPE_AXIS_DIM = HEAD_DIM // 2
ROPE_THETA = 10000.0
SC_WIDTH = D_MODEL - ATT_WIDTH
CONV_WIDTH = 3
C_HEADS = 8
C_DK = D_MODEL // C_HEADS
C_DV = D_MODEL // C_HEADS
C_WIDTH = C_HEADS * C_DK
D_FF = 2816
EVEN_IN_WIDTH = ATT_WIDTH + 2 * KV_WIDTH + 3 * SC_WIDTH
ALPHA = (2 * DEPTH) ** 0.25
EPS = 1e-6

P_TOK = BATCH * SEQ
S_TOK = DEC_BATCH * DEC_SEQ
N_TOK = P_TOK + S_TOK
MOD_ROWS = 8
SUBLANES = 8
TM = 512
HALO = SUBLANES
FF_CHUNK = 256
FFN_OUT_ROWS = 256
SCAN_CHUNK = 128
SCAN_BLOCK = DEC_SEQ
SCAN_SLOTS = 4
TQ = 512
VMEM_LIMIT = 56 * 1024 * 1024

assert P_TOK % TM == 0 and S_TOK % TM == 0 and DEC_SEQ % TM == 0 and TM % SEQ == 0 and DEC_SEQ % SEQ == 0
assert D_FF % FF_CHUNK == 0 and SEQ % SCAN_CHUNK == 0 and DEC_SEQ % SCAN_CHUNK == 0
assert SCAN_BLOCK % SEQ == 0 and P_TOK % SCAN_BLOCK == 0
P_BLOCKS = P_TOK // TM
N_BLOCKS = N_TOK // TM
S_BLOCKS_PER_SEQ = DEC_SEQ // TM

NT_DIMS = (((1,), (1,)), ((), ()))
TN_DIMS = (((0,), (0,)), ((), ()))


def _params(*sem):
    return pltpu.CompilerParams(dimension_semantics=sem, vmem_limit_bytes=VMEM_LIMIT)


def _mod_row(i):
    return jnp.where(i < P_BLOCKS, 0, 1 + (i - P_BLOCKS) // S_BLOCKS_PER_SEQ)


def _mod_spec():
    return pl.BlockSpec((1, 6, D_MODEL), lambda i: (_mod_row(i), 0, 0))


def _resident(shape):
    nd = len(shape)
    return pl.BlockSpec(shape, lambda *_: (0,) * nd, pipeline_mode=pl.Buffered(1))


def _split_specs(width):
    return (pl.BlockSpec((TM, width), lambda i: (jnp.minimum(i, P_BLOCKS - 1), 0)),
            pl.BlockSpec((TM, width), lambda i: (jnp.maximum(i - P_BLOCKS, 0), 0)))


def _sigmoid(x):
    return 0.5 * jnp.tanh(0.5 * x) + 0.5


def _silu(x):
    hx = 0.5 * x
    return hx * jnp.tanh(hx) + hx


def _layer_norm(x, g, b):
    mu = jnp.mean(x, axis=-1, keepdims=True)
    xc = x - mu
    var = jnp.mean(xc * xc, axis=-1, keepdims=True)
    return xc * lax.rsqrt(var + EPS) * g + b


def _residual_norm(x, y, gate, g, b):
    return _layer_norm(ALPHA * x + gate * y, g, b)


def _group_mean_matrix(group):
    idx = np.arange(256)
    return jnp.asarray((idx[:, None] // group == idx[None, :] // group) / group, dtype=BF16)


def _group_mean_square(x, gmat_ref):
    sq = (x * x).astype(BF16)
    n = x.shape[1] // 256
    parts = [jnp.dot(sq[:, 256 * j:256 * (j + 1)], gmat_ref[...], preferred_element_type=F32) for j in range(n)]
    return parts[0] if n == 1 else jnp.concatenate(parts, axis=1)


def _seq_pos(i, rows, first_row):
    seq_len = jnp.where(i < P_BLOCKS, SEQ, DEC_SEQ)
    r = lax.broadcasted_iota(jnp.int32, (rows, 1), 0) + (i * TM + first_row)
    return r & (seq_len - 1), seq_len


def _scale_tiles(x, factor, tiles):
    parts, pos = [], 0
    for t in sorted(tiles):
        lo = t * SUBLANES
        if lo > pos:
            parts.append(x[pos:lo])
        parts.append(x[lo:lo + SUBLANES] * factor[lo:lo + SUBLANES])
        pos = lo + SUBLANES
    if pos < x.shape[0]:
        parts.append(x[pos:])
    return jnp.concatenate(parts, axis=0)


MOD_TN = 1536


def _mod_kernel(cond_ref, w_ref, b_ref, o_ref):
    s = _silu(cond_ref[...])
    o_ref[...] = jnp.dot(s, w_ref[...], preferred_element_type=F32) + b_ref[...]


def _modulation(cond, w_mod, b_mod):
    n_out = 6 * D_MODEL
    return pl.pallas_call(
        _mod_kernel,
        grid=(DEPTH, n_out // MOD_TN),
        in_specs=[pl.BlockSpec((MOD_ROWS, D_MODEL), lambda l, j: (0, 0)),
                  pl.BlockSpec((None, D_MODEL, MOD_TN), lambda l, j: (l, 0, j)),
                  pl.BlockSpec((None, 1, MOD_TN), lambda l, j: (l, 0, j))],
        out_specs=pl.BlockSpec((None, MOD_ROWS, MOD_TN), lambda l, j: (l, 0, j)),
        out_shape=jax.ShapeDtypeStruct((DEPTH, MOD_ROWS, n_out), F32),
        compiler_params=_params("arbitrary", "arbitrary"),
        name="modulation",
    )(cond, w_mod, b_mod.reshape(DEPTH, 1, n_out))


def _rope_tables():
    t = np.arange(DEC_SEQ)
    half = ROPE_AXIS_DIM // 2
    inv = (ROPE_THETA ** (-np.arange(0, ROPE_AXIS_DIM, 2, dtype=np.float32) / ROPE_AXIS_DIM)).astype(np.float32)
    row = (t // GRID_W).astype(np.float32)[:, None] * inv
    col = (t % GRID_W).astype(np.float32)[:, None] * inv
    ang = np.concatenate([row, row, col, col], axis=1).astype(np.float32)
    sign = np.concatenate([-np.ones(half), np.ones(half)] * 2).astype(np.float32)
    cos = np.tile(np.cos(ang), (1, 128 // HEAD_DIM))
    sin = np.tile(np.sin(ang) * sign, (1, 128 // HEAD_DIM))
    return jnp.asarray(cos, F32), jnp.asarray(sin, F32)


def _rope(x, cos, sin):
    n = x.shape[1] // 128
    if n > 1:
        cos = jnp.concatenate([cos] * n, axis=1)
        sin = jnp.concatenate([sin] * n, axis=1)
    lane = lax.broadcasted_iota(jnp.int32, x.shape, 1)
    half = ROPE_AXIS_DIM // 2
    partner = jnp.where((lane & (ROPE_AXIS_DIM - 1)) < half,
                        pltpu.roll(x, x.shape[1] - half, 1), pltpu.roll(x, half, 1))
    return x * cos + partner * sin


def _attn_in_kernel(xp_ref, xs_ref, mod_ref, w_ref, qg_ref, kg_ref, gmat_ref, cos_ref, sin_ref,
                    q_ref, k_ref, v_ref, bg_ref, u_ref):
    i = pl.program_id(0)
    m = mod_ref[0]
    x = jnp.where(i < P_BLOCKS, xp_ref[...], xs_ref[...])
    h = (x * (1.0 + m[1:2]) + m[0:1]).astype(BF16)
    z = jnp.dot(h, w_ref[...], preferred_element_type=F32)
    q = z[:, :ATT_WIDTH]
    k = z[:, ATT_WIDTH:ATT_WIDTH + KV_WIDTH]
    o = ATT_WIDTH + 2 * KV_WIDTH
    q = q * lax.rsqrt(_group_mean_square(q, gmat_ref) + EPS) * qg_ref[...]
    kk = (k * k).astype(BF16)
    k_ms = jnp.dot(kk, gmat_ref[:KV_WIDTH, :KV_WIDTH], preferred_element_type=F32)
    k = k * lax.rsqrt(k_ms + EPS) * kg_ref[...]
    v_ref[...] = z[:, ATT_WIDTH + KV_WIDTH:o]
    bg_ref[...] = z[:, o:o + SC_WIDTH].astype(BF16)
    u_ref[...] = z[:, o + SC_WIDTH:o + 2 * SC_WIDTH] * z[:, o + 2 * SC_WIDTH:]

    @pl.when(i < P_BLOCKS)
    def _():
        q_ref[...] = (q * HEAD_DIM ** -0.5).astype(BF16)
        k_ref[...] = k

    @pl.when(i >= P_BLOCKS)
    def _():
        cos, sin = cos_ref[...], sin_ref[...]
        q_ref[...] = (_rope(q, cos, sin) * HEAD_DIM ** -0.5).astype(BF16)
        k_ref[...] = _rope(k, cos, sin)


def _attn_in(x_p, x_s, mod, w_in, q_gain, k_gain):
    cos, sin = _rope_tables()
    row = lambda i: (i, 0)
    rope_row = lambda i: (jnp.maximum(i - P_BLOCKS, 0) % S_BLOCKS_PER_SEQ, 0)
    tok = lambda w, dt: jax.ShapeDtypeStruct((N_TOK, w), dt)
    return pl.pallas_call(
        _attn_in_kernel,
        grid=(N_BLOCKS,),
        in_specs=[*_split_specs(D_MODEL), _mod_spec(),
                  _resident((D_MODEL, EVEN_IN_WIDTH)),
                  _resident((1, ATT_WIDTH)), _resident((1, KV_WIDTH)), _resident((256, 256)),
                  pl.BlockSpec((TM, 128), rope_row), pl.BlockSpec((TM, 128), rope_row)],
        out_specs=[pl.BlockSpec((TM, ATT_WIDTH), row), pl.BlockSpec((TM, KV_WIDTH), row),
                   pl.BlockSpec((TM, KV_WIDTH), row), pl.BlockSpec((TM, SC_WIDTH), row),
                   pl.BlockSpec((TM, SC_WIDTH), row)],
        out_shape=[tok(ATT_WIDTH, BF16), tok(KV_WIDTH, F32), tok(KV_WIDTH, F32), tok(SC_WIDTH, BF16),
                   tok(SC_WIDTH, F32)],
        compiler_params=_params("arbitrary"),
        name="attn_in",
    )(x_p, x_s, mod, w_in.astype(BF16), jnp.tile(q_gain, N_HEADS)[None], jnp.tile(k_gain, N_KV)[None],
      _group_mean_matrix(HEAD_DIM), cos, sin)


def _attend_kernel(q_ref, *refs, n_kv_sets):
    kv_refs, o_ref = refs[:2 * n_kv_sets], refs[-1]
    ks = [r[...].astype(BF16) for r in kv_refs[:n_kv_sets]]
    vs = [r[...].astype(BF16) for r in kv_refs[n_kv_sets:]]
    q = q_ref[...]
    for hd in range(N_HEADS):
        g = hd // Q_GROUP
        qh = q[:, hd * HEAD_DIM:(hd + 1) * HEAD_DIM]
        s = [lax.dot_general(qh, kx[:, g * HEAD_DIM:(g + 1) * HEAD_DIM], NT_DIMS, preferred_element_type=F32)
             for kx in ks]
        mx = functools.reduce(jnp.maximum, [jnp.max(sx, axis=-1, keepdims=True) for sx in s])
        p = [jnp.exp(sx - mx) for sx in s]
        den = functools.reduce(jnp.add, [jnp.sum(px, axis=-1, keepdims=True) for px in p])
        acc = functools.reduce(jnp.add, [
            jnp.dot(px.astype(BF16), vx[:, g * HEAD_DIM:(g + 1) * HEAD_DIM], preferred_element_type=F32)
            for px, vx in zip(p, vs)])
        o_ref[:, hd * HEAD_DIM:(hd + 1) * HEAD_DIM] = (acc / den).astype(BF16)


def _attend_prompt(q, k, v):
    blk = lambda w: pl.BlockSpec((SEQ, w), lambda b: (b, 0))
    return pl.pallas_call(
        functools.partial(_attend_kernel, n_kv_sets=1),
        grid=(BATCH,),
        in_specs=[blk(ATT_WIDTH), blk(KV_WIDTH), blk(KV_WIDTH)],
        out_specs=blk(ATT_WIDTH),
        out_shape=jax.ShapeDtypeStruct((P_TOK, ATT_WIDTH), BF16),
        compiler_params=_params("arbitrary"),
        name="attend_prompt",
    )(q, k, v)


def _attend_latent(q, k, v, ctx_k, ctx_v):
    nq = DEC_SEQ // TQ
    q_off, kv_off = P_TOK // TQ, P_TOK // DEC_SEQ
    lat = pl.BlockSpec((DEC_SEQ, KV_WIDTH), lambda b, j: (kv_off + b, 0))
    ctx = pl.BlockSpec((None, PAST_LEN, KV_WIDTH), lambda b, j: (b, 0, 0))
    return pl.pallas_call(
        functools.partial(_attend_kernel, n_kv_sets=2),
        grid=(DEC_BATCH, nq),
        in_specs=[pl.BlockSpec((TQ, ATT_WIDTH), lambda b, j: (q_off + b * nq + j, 0)), lat, ctx, lat, ctx],
        out_specs=pl.BlockSpec((TQ, ATT_WIDTH), lambda b, j: (b * nq + j, 0)),
        out_shape=jax.ShapeDtypeStruct((S_TOK, ATT_WIDTH), BF16),
        compiler_params=_params("arbitrary", "arbitrary"),
        name="attend_latent",
    )(q, k, ctx_k, v, ctx_v)


def _halo_specs(width):
    per = TM // HALO
    prev = pl.BlockSpec((HALO, width), lambda i: (jnp.maximum(i * per - 1, 0), 0))
    nxt = pl.BlockSpec((HALO, width), lambda i: (jnp.minimum((i + 1) * per, N_TOK // HALO - 1), 0))
    return prev, nxt


def _attn_out_kernel(xp_ref, xs_ref, mod_ref, attp_ref, atts_ref, bg_ref, u_ref, up_ref, un_ref, cw_ref, w_ref,
                     g_ref, b_ref, o_ref):
    i = pl.program_id(0)
    m = mod_ref[0]
    pos, seq_len = _seq_pos(i, TM, 0)
    r = lax.broadcasted_iota(jnp.int32, (TM, 1), 0)
    u = u_ref[...]
    u_prev = jnp.where(r == 0, up_ref[HALO - 1:HALO, :], pltpu.roll(u, 1, 0))
    u_next = jnp.where(r == TM - 1, un_ref[0:1, :], pltpu.roll(u, TM - 1, 0))
    u_prev = jnp.where(pos == 0, 0.0, u_prev)
    u_next = jnp.where(pos == seq_len - 1, 0.0, u_next)
    cw = cw_ref[...]
    sc = bg_ref[...].astype(F32) * (cw[0:1] * u_prev + cw[1:2] * u + cw[2:3] * u_next)
    att = jnp.where(i < P_BLOCKS, attp_ref[...], atts_ref[...])
    y = (jnp.dot(att, w_ref[:ATT_WIDTH, :], preferred_element_type=F32)
         + jnp.dot(sc.astype(BF16), w_ref[ATT_WIDTH:, :], preferred_element_type=F32))
    x = jnp.where(i < P_BLOCKS, xp_ref[...], xs_ref[...])
    o_ref[...] = _residual_norm(x, y, m[2:3], g_ref[...], b_ref[...])


def _attn_out(x_p, x_s, mod, att_p, att_s, bg, u, conv_w, w_out, g, b):
    row = lambda i: (i, 0)
    up, un = _halo_specs(SC_WIDTH)
    return pl.pallas_call(
        _attn_out_kernel,
        grid=(N_BLOCKS,),
        in_specs=[*_split_specs(D_MODEL), _mod_spec(),
                  *_split_specs(ATT_WIDTH), pl.BlockSpec((TM, SC_WIDTH), row),
                  pl.BlockSpec((TM, SC_WIDTH), row), up, un,
                  _resident((CONV_WIDTH, SC_WIDTH)), _resident((D_MODEL, D_MODEL)),
                  _resident((1, D_MODEL)), _resident((1, D_MODEL))],
        out_specs=pl.BlockSpec((TM, D_MODEL), row),
        out_shape=jax.ShapeDtypeStruct((N_TOK, D_MODEL), F32),
        compiler_params=_params("arbitrary"),
        name="attn_out",
    )(x_p, x_s, mod, att_p, att_s, bg, u, u, u, conv_w, w_out.astype(BF16), g[None], b[None])


_FFN_FIRST_TILES = tuple((HALO + n * SEQ) // SUBLANES for n in range(TM // SEQ))
_FFN_LAST_TILES = tuple((HALO + n * SEQ - 1) // SUBLANES for n in range(1, TM // SEQ + 1))


def _ffn_kernel(x_ref, xp_ref, xn_ref, mod_ref, wu_ref, cw_ref, wd_ref, g_ref, b_ref, *refs, split_out):
    o_refs, (h_ref, act_ref), u_refs = refs[:-6], refs[-6:-4], refs[-4:]
    u_ref = lambda slot, part: u_refs[2 * slot + part]
    i = pl.program_id(0)
    m = mod_ref[0]
    rows = TM + 2 * HALO
    scale, shift = 1.0 + m[4:5], m[3:4]
    h_ref[:HALO, :] = (xp_ref[...] * scale + shift).astype(BF16)
    h_ref[HALO:HALO + TM, :] = (x_ref[...] * scale + shift).astype(BF16)
    h_ref[HALO + TM:, :] = (xn_ref[...] * scale + shift).astype(BF16)
    pos, seq_len = _seq_pos(i, rows, -HALO)
    keep_prev = (pos != 0).astype(F32)
    keep_next = (pos != seq_len - 1).astype(F32)
    n_chunks = D_FF // FF_CHUNK

    def cols(c):
        return c * FF_CHUNK, D_FF + c * FF_CHUNK

    def up(c, slot):
        h = h_ref[...]
        for part, col in enumerate(cols(c)):
            u_ref(slot, part)[...] = jnp.dot(h, wu_ref[:, pl.ds(col, FF_CHUNK)], preferred_element_type=F32)

    def conv(slot, part, col):
        w = cw_ref[:, pl.ds(col, FF_CHUNK)]
        u = u_ref(slot, part)[...]
        u_prev = _scale_tiles(pltpu.roll(u, 1, 0), keep_prev, _FFN_FIRST_TILES)
        u_next = _scale_tiles(pltpu.roll(u, rows - 1, 0), keep_next, _FFN_LAST_TILES)
        return (w[0:1] * u_prev + w[1:2] * u + w[2:3] * u_next)[HALO:HALO + TM]

    def gate(c, slot):
        col_a, col_g = cols(c)
        act_ref[:, pl.ds(col_a, FF_CHUNK)] = (_silu(conv(slot, 1, col_g)) * conv(slot, 0, col_a)).astype(BF16)

    up(0, 0)
    for c in range(1, n_chunks):
        gate(c - 1, (c - 1) % 2)
        up(c, c % 2)
    gate(n_chunks - 1, (n_chunks - 1) % 2)
    ys = []
    for r0 in range(0, TM, FFN_OUT_ROWS):
        rs = slice(r0, r0 + FFN_OUT_ROWS)
        yr = jnp.dot(act_ref[rs, :], wd_ref[...], preferred_element_type=F32)
        ys.append(_residual_norm(x_ref[rs, :], yr, m[5:6], g_ref[...], b_ref[...]))
    y = jnp.concatenate(ys, axis=0)
    if split_out:
        @pl.when(i < P_BLOCKS)
        def _():
            o_refs[0][...] = y

        @pl.when(i >= P_BLOCKS)
        def _():
            o_refs[1][...] = y
    else:
        o_refs[0][...] = y


def _ffn(x, mod, layer, w_up, conv_w, w_down, g, b, split_out):
    row = lambda i: (i, 0)
    xp, xn = _halo_specs(D_MODEL)
    of_layer = lambda *shape: pl.BlockSpec((None, *shape), lambda i: (layer, 0, 0), pipeline_mode=pl.Buffered(1))
    if split_out:
        out_specs = list(_split_specs(D_MODEL))
        out_shape = [jax.ShapeDtypeStruct((P_TOK, D_MODEL), F32), jax.ShapeDtypeStruct((S_TOK, D_MODEL), F32)]
    else:
        out_specs = pl.BlockSpec((TM, D_MODEL), row)
        out_shape = jax.ShapeDtypeStruct((N_TOK, D_MODEL), F32)
    return pl.pallas_call(
        functools.partial(_ffn_kernel, split_out=split_out),
        grid=(N_BLOCKS,),
        in_specs=[pl.BlockSpec((TM, D_MODEL), row), xp, xn, _mod_spec(),
                  of_layer(D_MODEL, 2 * D_FF), of_layer(CONV_WIDTH, 2 * D_FF), of_layer(D_FF, D_MODEL),
                  _resident((1, D_MODEL)), _resident((1, D_MODEL))],
        out_specs=out_specs,
        out_shape=out_shape,
        scratch_shapes=[pltpu.VMEM((TM + 2 * HALO, D_MODEL), BF16),
                        pltpu.VMEM((TM, D_FF), BF16)] + [pltpu.VMEM((TM + 2 * HALO, FF_CHUNK), F32)] * 4,
        compiler_params=_params("arbitrary"),
        name="conv_ffn",
    )(x, x, x, mod, w_up, conv_w, w_down, g[None], b[None])


QIG_WIDTH = 3 * C_WIDTH
FF_WIDTH = 2 * C_WIDTH


def _hgrn_in_kernel(x_ref, mod_ref, w_ref, qig_ref, ff_ref):
    m = mod_ref[0]
    h = (x_ref[...] * (1.0 + m[1:2]) + m[0:1]).astype(BF16)
    for j in range(5):
        z = jnp.dot(h, w_ref[:, j * C_WIDTH:(j + 1) * C_WIDTH], preferred_element_type=F32)
        if j < 3:
            qig_ref[:, j * C_WIDTH:(j + 1) * C_WIDTH] = z.astype(BF16)
        else:
            ff_ref[:, (j - 3) * C_WIDTH:(j - 2) * C_WIDTH] = z


def _hgrn_in(x, mod, w_in):
    row = lambda i: (i, 0)
    return pl.pallas_call(
        _hgrn_in_kernel,
        grid=(N_BLOCKS,),
        in_specs=[pl.BlockSpec((TM, D_MODEL), row), _mod_spec(), _resident((D_MODEL, 5 * C_WIDTH))],
        out_specs=[pl.BlockSpec((TM, QIG_WIDTH), row), pl.BlockSpec((TM, FF_WIDTH), row)],
        out_shape=[jax.ShapeDtypeStruct((N_TOK, QIG_WIDTH), BF16), jax.ShapeDtypeStruct((N_TOK, FF_WIDTH), F32)],
        compiler_params=_params("arbitrary"),
        name="hgrn_in",
    )(x, mod, w_in.astype(BF16))


SCAN_LEVELS = tuple(1 << j for j in range(SCAN_CHUNK.bit_length() - 1))
FINE_LEVELS = tuple(h for h in SCAN_LEVELS if h < SUBLANES)
N_LEVELS = len(SCAN_LEVELS)
DIAG_CODE = 2 * N_LEVELS


def _is_query_side(t, half, forward):
    right = (t & half) != 0
    return right if forward else ~right


def _decay_matrix(forward):
    idx = np.arange(SCAN_CHUNK)
    t, s = idx[:, None], idx[None, :]
    cum = (s <= t) if forward else (s >= t)
    mats = [cum.astype(np.float32)]
    for half in FINE_LEVELS:
        ref = (idx // (2 * half)) * (2 * half) + (half - 1 if forward else half)
        sign = np.where(_is_query_side(idx, half, forward), 1.0, -1.0)[:, None]
        mats.append(sign * (cum.astype(np.float32) - cum[ref].astype(np.float32)))
    mat = np.concatenate(mats, axis=0)
    return jnp.asarray(np.concatenate([mat, mat], axis=1), dtype=BF16)


def _pair_codes():
    idx = np.arange(SCAN_CHUNK)
    t, s = idx[:, None], idx[None, :]
    lev = np.floor(np.log2(np.maximum(t ^ s, 1))).astype(np.int32)
    code = np.where(t > s, lev, np.where(t < s, N_LEVELS + lev, DIAG_CODE))
    return jnp.asarray(code, dtype=jnp.int32)


def _decay_terms(f_raw, lb):
    f = lb + (1.0 - lb) * _sigmoid(f_raw)
    g = jnp.log2(f)
    hi = g.astype(BF16)
    lo = (g - hi.astype(F32)).astype(BF16)
    return jnp.concatenate([hi, lo], axis=0), 1.0 - f


def _fine_products(q, k, expo, forward):
    C = SCAN_CHUNK
    t = lax.broadcasted_iota(jnp.int32, (C, 1), 0)
    prods = []
    for j, half in enumerate(FINE_LEVELS):
        w = jnp.exp2(expo[(j + 1) * C:(j + 2) * C])
        x = (jnp.where(_is_query_side(t, half, forward), q, k) * w).astype(BF16)
        prods.append(lax.dot_general(x, x, NT_DIMS, preferred_element_type=F32))
    return prods


def _select_fine(prods, code, forward, scores):
    base = 0 if forward else N_LEVELS
    for j, prod in enumerate(prods):
        scores = jnp.where(code == base + j, prod, scores)
    return scores


def _coarse_products(q, k, cum, forward):
    C = SCAN_CHUNK
    out = []
    for j, half in enumerate(SCAN_LEVELS):
        if half in FINE_LEVELS:
            continue
        blk = 2 * half
        xq, xk, q_rows = [], [], []
        zero = jnp.zeros((half, C_DK), F32)
        for n in range(C // blk):
            lo, hi = slice(n * blk, n * blk + half), slice(n * blk + half, (n + 1) * blk)
            if forward:
                ref = cum[n * blk + half - 1:n * blk + half]
                xk += [k[lo] * jnp.exp2(ref - cum[lo]), zero]
                xq.append(q[hi] * jnp.exp2(cum[hi] - ref))
                q_rows.append(n * blk + half)
            else:
                ref = cum[n * blk + half:n * blk + half + 1]
                xq.append(q[lo] * jnp.exp2(cum[lo] - ref))
                xk += [zero, k[hi] * jnp.exp2(ref - cum[hi])]
                q_rows.append(n * blk)
        prod = lax.dot_general(jnp.concatenate(xq, axis=0).astype(BF16), jnp.concatenate(xk, axis=0).astype(BF16),
                               NT_DIMS, preferred_element_type=F32)
        out.append((j, half, q_rows, prod))
    return out


def _select_coarse(products, code, forward, rows):
    base = 0 if forward else N_LEVELS
    for j, half, q_rows, prod in products:
        for n, r0 in enumerate(q_rows):
            for i in range(half // SUBLANES):
                rb, src = r0 // SUBLANES + i, n * half + SUBLANES * i
                own = code[SUBLANES * rb:SUBLANES * (rb + 1)] == base + j
                rows[rb] = jnp.where(own, prod[src:src + SUBLANES], rows[rb])
    return rows


def _hgrn_scan_kernel(*refs, sub_len, has_init):
    q_ref, v_ref, ff_ref, fb_ref, lbl_ref, dmf_ref, dmb_ref, code_ref = refs[:8]
    refs = refs[8:]
    if has_init:
        s0_ref, refs = refs[0], refs[1:]
    o_ref, refs = refs[0], refs[1:]
    if not has_init:
        sout_ref, refs = refs[0], refs[1:]
    (sc_ref, qdf_ref, qdb_ref, decf_ref, decb_ref, incf_ref, incb_ref, stf_ref, stb_ref,
     expo_ref, k_ref, kd_ref, g2_ref) = refs
    C = SCAN_CHUNK
    n_chunks = SCAN_BLOCK // C
    sub_chunks = sub_len // C

    lg = lbl_ref[...]
    ex = jnp.exp(lg - jnp.max(lg, axis=0, keepdims=True))
    prob = ex / jnp.sum(ex, axis=0, keepdims=True)
    lb = jnp.sum(prob[1:DEPTH], axis=0)
    lb_f, lb_b = lb[0:1], lb[1:2]
    code = code_ref[...]

    def rows_of(c):
        return pl.ds(c * C if isinstance(c, int) else pl.multiple_of(c * C, C), C)

    def increments(c4):
        v4, slot = v_ref[rows_of(c4), :], c4 % SCAN_SLOTS
        incf_ref[c4] = lax.dot_general(v4, kd_ref[slot, 0], TN_DIMS, preferred_element_type=F32)
        incb_ref[c4] = lax.dot_general(v4, kd_ref[slot, 1], TN_DIMS, preferred_element_type=F32)

    def pair_weights(c2):
        r2, slot = rows_of(c2), c2 % SCAN_SLOTS
        q = q_ref[r2, :].astype(F32)
        expo_f, expo_b = expo_ref[slot, 0], expo_ref[slot, 1]
        k_f, k_b = k_ref[slot, 0], k_ref[slot, 1]
        cum_f, cum_b = expo_f[:C], expo_b[:C]
        fine_f, fine_b = _fine_products(q, k_f, expo_f, True), _fine_products(q, k_b, expo_b, False)
        coarse_f, coarse_b = _coarse_products(q, k_f, cum_f, True), _coarse_products(q, k_b, cum_b, False)
        last_f, last_b = cum_f[C - 1:C, :], cum_b[0:1, :]
        qdf_ref[r2, :] = (q * jnp.exp2(cum_f)).astype(BF16)
        qdb_ref[r2, :] = (q * jnp.exp2(cum_b)).astype(BF16)
        decf_ref[pl.ds(c2, 1), :] = jnp.exp2(last_f)
        decb_ref[pl.ds(c2, 1), :] = jnp.exp2(last_b)
        kd_ref[slot, 0] = (k_f * jnp.exp2(last_f - cum_f)).astype(BF16)
        kd_ref[slot, 1] = (k_b * jnp.exp2(last_b - cum_b)).astype(BF16)
        same_row = jnp.sum(q * (k_f + k_b), axis=-1, keepdims=True)
        scores = _select_fine(fine_f, code, True, jnp.zeros((C, C), F32))
        scores = _select_fine(fine_b, code, False, scores)
        rows = [scores[SUBLANES * n:SUBLANES * (n + 1)] for n in range(C // SUBLANES)]
        rows = _select_coarse(coarse_f, code, True, rows)
        rows = _select_coarse(coarse_b, code, False, rows)
        sc_ref[c2] = jnp.where(code == DIAG_CODE, same_row, jnp.concatenate(rows, axis=0)).astype(BF16)

    def cumulate(c):
        slot = c % SCAN_SLOTS
        expo_ref[slot, 0] = jnp.dot(dmf_ref[...], g2_ref[slot, 0], preferred_element_type=F32)
        expo_ref[slot, 1] = jnp.dot(dmb_ref[...], g2_ref[slot, 1], preferred_element_type=F32)

    def decays(c):
        r1, slot = rows_of(c), c % SCAN_SLOTS
        g2_ref[slot, 0], k_ref[slot, 0] = _decay_terms(ff_ref[r1, :], lb_f)
        g2_ref[slot, 1], k_ref[slot, 1] = _decay_terms(fb_ref[r1, :], lb_b)

    for t in range(n_chunks + 3):
        if 0 <= t - 1 < n_chunks:
            cumulate(t - 1)
        if 0 <= t - 2 < n_chunks:
            pair_weights(t - 2)
        if t < n_chunks:
            decays(t)
        if 0 <= t - 3 < n_chunks:
            increments(t - 3)

    for sub in range(SCAN_BLOCK // sub_len):
        first = sub * sub_chunks
        if has_init:
            init = (s0_ref[0].T, s0_ref[1].T)
        else:
            init = (jnp.zeros((C_DV, C_DK), F32),) * 2

        st_f, st_b = init
        for c in range(sub_chunks):
            cf, cb = first + c, first + sub_chunks - 1 - c
            stf_ref[cf] = st_f.astype(BF16)
            stb_ref[cb] = st_b.astype(BF16)
            st_f = st_f * decf_ref[pl.ds(cf, 1), :] + incf_ref[cf]
            st_b = st_b * decb_ref[pl.ds(cb, 1), :] + incb_ref[cb]
        if not has_init:
            sout_ref[sub, 0] = st_f.T
            sout_ref[sub, 1] = st_b.T

    for c in range(n_chunks):
        r = rows_of(c)
        o_ref[r, :] = (jnp.dot(sc_ref[c], v_ref[r, :], preferred_element_type=F32)
                       + lax.dot_general(qdf_ref[r, :], stf_ref[c], NT_DIMS, preferred_element_type=F32)
                       + lax.dot_general(qdb_ref[r, :], stb_ref[c], NT_DIMS, preferred_element_type=F32))


def _hgrn_scan(qig, ff, lb_logits, sub_len, row_off, n_rows, s0):
    n_blocks = n_rows // SCAN_BLOCK
    blk_off = row_off // SCAN_BLOCK
    n_sub = SCAN_BLOCK // sub_len
    heads = C_WIDTH // C_DK
    n_chunks = SCAN_BLOCK // SCAN_CHUNK
    col = lambda part: pl.BlockSpec((SCAN_BLOCK, C_DK), lambda n, h: (blk_off + n, part * heads + h))
    dmat_rows = (1 + len(FINE_LEVELS)) * SCAN_CHUNK
    in_specs = [col(0), col(1), col(0), col(1),
                pl.BlockSpec((DEPTH, 2, C_DK), lambda n, h: (0, 0, h)),
                pl.BlockSpec((dmat_rows, 2 * SCAN_CHUNK), lambda n, h: (0, 0)),
                pl.BlockSpec((dmat_rows, 2 * SCAN_CHUNK), lambda n, h: (0, 0)),
                pl.BlockSpec((SCAN_CHUNK, SCAN_CHUNK), lambda n, h: (0, 0))]
    args = [qig, qig, ff, ff, lb_logits, _decay_matrix(True), _decay_matrix(False), _pair_codes()]
    state_spec = lambda n_seq: pl.BlockSpec((n_seq, 2, None, C_DK, C_DV), lambda n, h: (n, 0, h, 0, 0))
    if s0 is not None:
        assert n_sub == 1
        in_specs.append(pl.BlockSpec((None, 2, None, C_DK, C_DV), lambda n, h: (n, 0, h, 0, 0)))
        args.append(s0)
    out_specs = [pl.BlockSpec((SCAN_BLOCK, C_DV), lambda n, h: (n, h))]
    out_shape = [jax.ShapeDtypeStruct((n_rows, C_WIDTH), F32)]
    if s0 is None:
        out_specs.append(state_spec(n_sub))
        out_shape.append(jax.ShapeDtypeStruct((n_blocks * n_sub, 2, heads, C_DK, C_DV), F32))
    chunk_mats = lambda dt: pltpu.VMEM((n_chunks, SCAN_CHUNK, SCAN_CHUNK), dt)
    return pl.pallas_call(
        functools.partial(_hgrn_scan_kernel, sub_len=sub_len, has_init=s0 is not None),
        grid=(n_blocks, heads),
        in_specs=in_specs, out_specs=out_specs, out_shape=out_shape,
        scratch_shapes=[chunk_mats(BF16), pltpu.VMEM((SCAN_BLOCK, C_DK), BF16), pltpu.VMEM((SCAN_BLOCK, C_DK), BF16),
                        pltpu.VMEM((n_chunks, C_DK), F32), pltpu.VMEM((n_chunks, C_DK), F32),
                        chunk_mats(F32), chunk_mats(F32), chunk_mats(BF16), chunk_mats(BF16),
                        pltpu.VMEM((SCAN_SLOTS, 2, dmat_rows, C_DK), F32),
                        pltpu.VMEM((SCAN_SLOTS, 2, SCAN_CHUNK, C_DK), F32),
                        pltpu.VMEM((SCAN_SLOTS, 2, SCAN_CHUNK, C_DK), BF16),
                        pltpu.VMEM((SCAN_SLOTS, 2, 2 * SCAN_CHUNK, C_DK), BF16)],
        compiler_params=_params("arbitrary", "arbitrary"),
        name="hgrn_scan_init" if s0 is not None else "hgrn_scan_zero",
    )(*args)


def _hgrn_out_kernel(x_ref, mod_ref, op_ref, os_ref, gate_ref, ng_ref, gmat_ref, w_ref, g_ref, b_ref, out_ref):
    m = mod_ref[0]
    o = jnp.where(pl.program_id(0) < P_BLOCKS, op_ref[...], os_ref[...])
    o = o * lax.rsqrt(_group_mean_square(o, gmat_ref) + EPS) * ng_ref[...] * _silu(gate_ref[...].astype(F32))
    y = jnp.dot(o.astype(BF16), w_ref[...], preferred_element_type=F32)
    out_ref[...] = _residual_norm(x_ref[...], y, m[2:3], g_ref[...], b_ref[...])


def _hgrn_out(x, mod, o_p, o_s, qig, norm_g, w_out, g, b):
    row = lambda i: (i, 0)
    return pl.pallas_call(
        _hgrn_out_kernel,
        grid=(N_BLOCKS,),
        in_specs=[pl.BlockSpec((TM, D_MODEL), row), _mod_spec(),
                  *_split_specs(C_WIDTH), pl.BlockSpec((TM, C_WIDTH), lambda i: (i, 2)),
                  _resident((1, C_WIDTH)), _resident((256, 256)), _resident((C_WIDTH, D_MODEL)),
                  _resident((1, D_MODEL)), _resident((1, D_MODEL))],
        out_specs=pl.BlockSpec((TM, D_MODEL), row),
        out_shape=jax.ShapeDtypeStruct((N_TOK, D_MODEL), F32),
        compiler_params=_params("arbitrary"),
        name="hgrn_out",
    )(x, mod, o_p, o_s, qig, jnp.tile(norm_g, C_HEADS)[None], _group_mean_matrix(C_DV), w_out.astype(BF16), g[None],
      b[None])


def kernel(x_prompt, x_sample, cache_k, cache_v, state_hgrn, c, c_ctx, w_mod, b_mod, ln1_g, ln1_b, ln2_g, ln2_b,
           attn_w_in, attn_q_gain, attn_k_gain, sconv_w, attn_w_out, hgrn_w_in, hgrn_lb_logits, hgrn_norm_g,
           hgrn_w_out, ffn_w_up, ffn_conv_w, ffn_w_down):
    x_p = x_prompt.reshape(P_TOK, D_MODEL)
    x_s = x_sample.reshape(S_TOK, D_MODEL)
    cond = jnp.concatenate([c_ctx[None], c, jnp.zeros((MOD_ROWS - 1 - DEC_BATCH, D_MODEL), F32)], axis=0)
    mod = _modulation(cond, w_mod, b_mod).reshape(DEPTH, MOD_ROWS, 6, D_MODEL)

    q, k, v, bg, u = _attn_in(x_p, x_s, mod[0], attn_w_in[0], attn_q_gain[0], attn_k_gain[0])
    att_p = _attend_prompt(q, k, v)
    att_s = _attend_latent(q, k, v, cache_k[:, 0].reshape(DEC_BATCH, PAST_LEN, KV_WIDTH),
                           cache_v[:, 0].reshape(DEC_BATCH, PAST_LEN, KV_WIDTH))
    x = _attn_out(x_p, x_s, mod[0], att_p, att_s, bg, u, sconv_w[0], attn_w_out[0], ln1_g[0], ln1_b[0])
    w_up, w_down = ffn_w_up.astype(BF16), ffn_w_down.astype(BF16)
    x = _ffn(x, mod[0], 0, w_up, ffn_conv_w, w_down, ln2_g[0], ln2_b[0], split_out=False)
    new_k = k[:P_TOK].reshape(BATCH, 1, SEQ, N_KV, HEAD_DIM)
    new_v = v[:P_TOK].reshape(BATCH, 1, SEQ, N_KV, HEAD_DIM)

    qig, ff = _hgrn_in(x, mod[1], hgrn_w_in[0])
    o_p, s_new = _hgrn_scan(qig, ff, hgrn_lb_logits, SEQ, 0, P_TOK, None)
    o_s, = _hgrn_scan(qig, ff, hgrn_lb_logits, DEC_SEQ, P_TOK, S_TOK, state_hgrn[:, 0])
    x = _hgrn_out(x, mod[1], o_p, o_s, qig, hgrn_norm_g[0], hgrn_w_out[0], ln1_g[1], ln1_b[1])
    y_p, y_s = _ffn(x, mod[1], 1, w_up, ffn_conv_w, w_down, ln2_g[1], ln2_b[1], split_out=True)

    return (y_p.reshape(BATCH, SEQ, D_MODEL), y_s.reshape(DEC_BATCH, DEC_SEQ, D_MODEL), new_k, new_v, s_new[:, None])
```
